```python
import math
import jax, jax.numpy as jnp
from jax import lax
import numpy as np

D_MODEL = 4096
BATCH = 4
SEQ = 2048
DEPTH = 1
DEC_BATCH = 128
DEC_SEQ = 1
PAST_LEN = 16384
PAGE_SIZE = 128

MIX_WIDTH = D_MODEL
HGRN_DK = 128
HGRN_DV = 128
HGRN_HEADS = (MIX_WIDTH // 2) // HGRN_DV
QK_W = HGRN_HEADS * HGRN_DK
V_W = HGRN_HEADS * HGRN_DV
S5_WIDTH = MIX_WIDTH - V_W
S5_GROUP = 16
S5_GROUPS = S5_WIDTH // S5_GROUP
S5_STATE = 64
IN_COLS = 2 * QK_W + 2 * V_W + S5_WIDTH
CHUNK = 64
N_EXPERTS = 32
TOP_K = 4
D_FF = D_MODEL
SWIGLU_LIMIT = 7.0
SWIGLU_ALPHA = 1.702
PLE_DIM = 256
EPS = 1e-6

kernel_name = "hgrn2_s5_parallel_moe_decode_step"


def rmsnorm(x, g):
    xf = x.astype(jnp.float32)
    y = xf * lax.rsqrt(jnp.mean(xf * xf, axis=-1, keepdims=True) + EPS)
    return (y * g.astype(jnp.float32)).astype(x.dtype)


def hgrn2_chunked(q, k, v, logf, s0):
    bsz, T, H, dk = q.shape
    c = min(CHUNK, T)
    n = -(-T // c)
    pad = n * c - T
    if pad:
        cfg = ((0, 0), (0, pad), (0, 0), (0, 0))
        q, k, v, logf = [jnp.pad(a, cfg) for a in (q, k, v, logf)]

    def chunks(a):
        return a.reshape(bsz, n, c, H, a.shape[-1]).transpose(1, 0, 3, 2, 4)

    causal = jnp.tril(jnp.ones((c, c), dtype=bool))[:, :, None]

    def step(S, inp):
        qc, kc, vc, lc = inp
        b = jnp.cumsum(lc, axis=2)
        rel = jnp.where(causal, b[:, :, :, None, :] - b[:, :, None, :, :], -jnp.inf)
        att = jnp.einsum('bhtsd,bhsd->bhts', qc[:, :, :, None, :] * jnp.exp(rel), kc)
        o = jnp.einsum('bhts,bhsv->bhtv', att, vc) + jnp.einsum('bhtd,bhdv->bhtv', qc * jnp.exp(b), S)
        b_last = b[:, :, -1:, :]
        S = jnp.exp(b_last[:, :, 0, :])[..., None] * S + jnp.einsum('bhsd,bhsv->bhdv', kc * jnp.exp(b_last - b), vc)
        return S, o

    S, o = lax.scan(step, s0, (chunks(q), chunks(k), chunks(v), chunks(logf)))
    o = o.transpose(1, 0, 3, 2, 4).reshape(bsz, n * c, H, v.shape[-1])[:, :T]
    return o, S


def _complex_combine(e1, e2):
    a1r, a1i, b1r, b1i = e1
    a2r, a2i, b2r, b2i = e2
    return (a1r * a2r - a1i * a2i,
            a1r * a2i + a1i * a2r,
            a2r * b1r - a2i * b1i + b2r,
            a2r * b1i + a2i * b1r + b2i)


def s5_scan(u, a_re, a_im, log_dt, b_re, b_im, c_re, c_im, d_skip, x0_re, x0_im):
    f32 = jnp.float32
    a_re, a_im, log_dt = a_re.astype(f32), a_im.astype(f32), log_dt.astype(f32)
    b_re, b_im, c_re, c_im = b_re.astype(f32), b_im.astype(f32), c_re.astype(f32), c_im.astype(f32)
    T = u.shape[1]
    dt = jnp.exp(log_dt)
    ar, ai = a_re * dt, a_im * dt
    mag = jnp.exp(ar)
    abar_re, abar_im = mag * jnp.cos(ai), mag * jnp.sin(ai)
    den = a_re * a_re + a_im * a_im
    nr = abar_re - 1.0
    cr = (nr * a_re + abar_im * a_im) / den
    ci = (abar_im * a_re - nr * a_im) / den
    bb_re = cr[..., None] * b_re - ci[..., None] * b_im
    bb_im = cr[..., None] * b_im + ci[..., None] * b_re
    bu_re = jnp.einsum('btgh,gph->btgp', u, bb_re)
    bu_im = jnp.einsum('btgh,gph->btgp', u, bb_im)
    shape = bu_re.shape
    elems = (jnp.broadcast_to(abar_re, shape), jnp.broadcast_to(abar_im, shape), bu_re, bu_im)
    _, _, xr, xi = lax.associative_scan(_complex_combine, elems, axis=1)
    if x0_re is not None:
        steps = jnp.arange(1, T + 1, dtype=f32)[:, None, None]
        pm = jnp.exp(steps * ar)
        pr, pim = pm * jnp.cos(steps * ai), pm * jnp.sin(steps * ai)
        x0r = x0_re.astype(f32)[:, None]
        x0i = x0_im.astype(f32)[:, None]
        xr, xi = xr + pr * x0r - pim * x0i, xi + pr * x0i + pim * x0r
    y = (jnp.einsum('btgp,ghp->btgh', xr, c_re) - jnp.einsum('btgp,ghp->btgh', xi, c_im)
         + d_skip.astype(f32) * u)
    return y, xr[:, -1], xi[:, -1]


def token_mixers(x, hgrn_s0, s5_x0_re, s5_x0_im, lb, g_mix, w_in, g_onorm,
                 a_re, a_im, log_dt, b_re, b_im, c_re, c_im, d_skip,
                 glu_w1, glu_b1, glu_w2, glu_b2, w_out):
    f32 = jnp.float32
    bsz, T, _ = x.shape
    z = rmsnorm(x, g_mix) @ w_in
    q, fz, iv, gz, u = jnp.split(z, [QK_W, 2 * QK_W, 2 * QK_W + V_W, 2 * QK_W + 2 * V_W], axis=-1)
    fz = fz.astype(f32).reshape(bsz, T, HGRN_HEADS, HGRN_DK)
    lbh = lb.reshape(HGRN_HEADS, HGRN_DK)
    f = lbh + (1.0 - lbh) * jax.nn.sigmoid(fz)
    k = (1.0 - lbh) * jax.nn.sigmoid(-fz)
    logf = jnp.log(f)
    qh = q.astype(f32).reshape(bsz, T, HGRN_HEADS, HGRN_DK)
    vh = iv.astype(f32).reshape(bsz, T, HGRN_HEADS, HGRN_DV)
    if hgrn_s0 is None:
        hgrn_s0 = jnp.zeros((bsz, HGRN_HEADS, HGRN_DK, HGRN_DV), f32)
    o, S = hgrn2_chunked(qh, k, vh, logf, hgrn_s0.astype(f32))
    o = rmsnorm(o, g_onorm.reshape(HGRN_HEADS, HGRN_DV))
    gh = gz.astype(f32).reshape(bsz, T, HGRN_HEADS, HGRN_DV)
    o_a = (o * jax.nn.silu(gh)).reshape(bsz, T, V_W)
    ug = u.astype(f32).reshape(bsz, T, S5_GROUPS, S5_GROUP)
    y, xr, xi = s5_scan(ug, a_re, a_im, log_dt, b_re, b_im, c_re, c_im, d_skip, s5_x0_re, s5_x0_im)
    yg = jax.nn.gelu(y)
    glu = ((jnp.einsum('btgh,ghk->btgk', yg, glu_w1.astype(f32)) + glu_b1.astype(f32))
           * jax.nn.sigmoid(jnp.einsum('btgh,ghk->btgk', yg, glu_w2.astype(f32)) + glu_b2.astype(f32)))
    o_b = glu.reshape(bsz, T, S5_WIDTH)
    mix = jnp.concatenate([o_a, o_b], axis=-1).astype(x.dtype) @ w_out
    return mix, S, xr, xi


def moe(h, w_router, b_router, w_gate_up, b_gate_up, w_down, b_down):
    f32 = jnp.float32
    logits = h.astype(f32) @ w_router.astype(f32) + b_router.astype(f32)
    top_v, top_i = lax.top_k(logits, TOP_K)
    top_w = jax.nn.softmax(top_v, axis=-1)
    gates = jnp.einsum('tk,tke->te', top_w, jax.nn.one_hot(top_i, N_EXPERTS, dtype=f32))
    out = jnp.zeros(h.shape, f32)
    for e in range(N_EXPERTS):
        gu = h @ w_gate_up[e] + b_gate_up[e]
        gate, up = gu[:, :D_FF], gu[:, D_FF:]
        gate = jnp.minimum(gate, SWIGLU_LIMIT)
        up = jnp.clip(up, -SWIGLU_LIMIT, SWIGLU_LIMIT)
        act = (up + 1.0) * gate * jax.nn.sigmoid(SWIGLU_ALPHA * gate)
        out = out + gates[:, e:e + 1] * (act @ w_down[e] + b_down[e]).astype(f32)
    return out.astype(h.dtype)


def ple(h, p, g_ple, w_ple_gate, w_ple_proj):
    gate = jax.nn.sigmoid((rmsnorm(h, g_ple) @ w_ple_gate).astype(jnp.float32))
    emb = (p.astype(h.dtype) @ w_ple_proj).astype(jnp.float32)
    return (gate * emb).astype(h.dtype)


def setup_inputs(seed: int = 0) -> dict:
    key = jax.random.key(seed)
    ks = iter(jax.random.split(key, 48))
    f32 = jnp.float32
    nrm = lambda shape, s: jax.random.normal(next(ks), shape, f32) * s
    L, G, P, HG, E = DEPTH, S5_GROUPS, S5_STATE, S5_GROUP, N_EXPERTS
    d = {}
    d['x_prompt'] = nrm((BATCH, SEQ, D_MODEL), 1.0)
    d['x_sample'] = nrm((DEC_BATCH, DEC_SEQ, D_MODEL), 1.0)
    d['state_hgrn'] = nrm((L, DEC_BATCH, HGRN_HEADS, HGRN_DK, HGRN_DV), 0.3)
    d['state_s5_re'] = nrm((L, DEC_BATCH, G, P), 0.5)
    d['state_s5_im'] = nrm((L, DEC_BATCH, G, P), 0.5)
    d['p_prompt'] = nrm((L, BATCH, SEQ, PLE_DIM), 1.0)
    d['p_sample'] = nrm((L, DEC_BATCH, DEC_SEQ, PLE_DIM), 1.0)
    d['g_mix'] = 1.0 + nrm((L, D_MODEL), 0.02)
    d['w_in'] = nrm((L, D_MODEL, IN_COLS), D_MODEL ** -0.5)
    d['hgrn_gamma'] = nrm((L + 1, QK_W), 0.5)
    d['g_onorm'] = 1.0 + nrm((L, V_W), 0.02)
    d['s5_a_re'] = -0.5 + nrm((L, G, P), 0.01)
    d['s5_a_im'] = math.pi * jnp.arange(P, dtype=f32) + nrm((L, G, P), 0.01)
    d['s5_log_dt'] = jax.random.uniform(next(ks), (L, G, P), f32, math.log(1e-3), math.log(1e-1))
    d['s5_b_re'] = nrm((L, G, P, HG), (2 * HG) ** -0.5)
    d['s5_b_im'] = nrm((L, G, P, HG), (2 * HG) ** -0.5)
    d['s5_c_re'] = nrm((L, G, HG, P), P ** -0.5)
    d['s5_c_im'] = nrm((L, G, HG, P), P ** -0.5)
    d['s5_d'] = nrm((L, G, HG), 1.0)
    d['glu_w1'] = nrm((L, G, HG, HG), HG ** -0.5)
    d['glu_b1'] = nrm((L, G, HG), 0.01)
    d['glu_w2'] = nrm((L, G, HG, HG), HG ** -0.5)
    d['glu_b2'] = nrm((L, G, HG), 0.01)
    d['w_out'] = nrm((L, MIX_WIDTH, D_MODEL), MIX_WIDTH ** -0.5)
    d['g_ffn'] = 1.0 + nrm((L, D_MODEL), 0.02)
    d['w_router'] = nrm((L, D_MODEL, E), D_MODEL ** -0.5)
    d['b_router'] = nrm((L, E), 0.01)
    d['w_gate_up'] = nrm((L, E, D_MODEL, 2 * D_FF), D_MODEL ** -0.5)
    d['b_gate_up'] = nrm((L, E, 2 * D_FF), 0.01)
    d['w_down'] = nrm((L, E, D_FF, D_MODEL), D_FF ** -0.5)
    d['b_down'] = nrm((L, E, D_MODEL), 0.01)
    d['g_ple'] = 1.0 + nrm((L, D_MODEL), 0.02)
    d['w_ple_gate'] = nrm((L, D_MODEL, D_MODEL), D_MODEL ** -0.5)
    d['w_ple_proj'] = nrm((L, PLE_DIM, D_MODEL), PLE_DIM ** -0.5)
    d['g_final'] = 1.0 + nrm((D_MODEL,), 0.02)
    return d


def reference(x_prompt, x_sample, state_hgrn, state_s5_re, state_s5_im, p_prompt, p_sample,
              g_mix, w_in, hgrn_gamma, g_onorm, s5_a_re, s5_a_im, s5_log_dt, s5_b_re, s5_b_im,
              s5_c_re, s5_c_im, s5_d, glu_w1, glu_b1, glu_w2, glu_b2, w_out, g_ffn,
              w_router, b_router, w_gate_up, b_gate_up, w_down, b_down,
              g_ple, w_ple_gate, w_ple_proj, g_final):
    lb_all = jnp.cumsum(jax.nn.softmax(hgrn_gamma.astype(jnp.float32), axis=0), axis=0)
    hp, hs = x_prompt, x_sample
    hgrn_p, s5r_p, s5i_p, hgrn_s, s5r_s, s5i_s = [], [], [], [], [], []
    n_prompt = hp.shape[0] * hp.shape[1]
    for i in range(DEPTH):
        lw = (g_mix[i], w_in[i], g_onorm[i], s5_a_re[i], s5_a_im[i], s5_log_dt[i], s5_b_re[i], s5_b_im[i],
              s5_c_re[i], s5_c_im[i], s5_d[i], glu_w1[i], glu_b1[i], glu_w2[i], glu_b2[i], w_out[i])
        mix_p, Sp, rp, ip = token_mixers(hp, None, None, None, lb_all[i], *lw)
        mix_s, Ss, rs, is_ = token_mixers(hs, state_hgrn[i], state_s5_re[i], state_s5_im[i], lb_all[i], *lw)
        hgrn_p.append(Sp); s5r_p.append(rp); s5i_p.append(ip)
        hgrn_s.append(Ss); s5r_s.append(rs); s5i_s.append(is_)
        hp = hp + mix_p
        hs = hs + mix_s
        tokens = jnp.concatenate([rmsnorm(hp, g_ffn[i]).reshape(-1, D_MODEL),
                                  rmsnorm(hs, g_ffn[i]).reshape(-1, D_MODEL)], axis=0)
        ffn = moe(tokens, w_router[i], b_router[i], w_gate_up[i], b_gate_up[i], w_down[i], b_down[i])
        hp = hp + ffn[:n_prompt].reshape(hp.shape)
        hs = hs + ffn[n_prompt:].reshape(hs.shape)
        hp = hp + ple(hp, p_prompt[i], g_ple[i], w_ple_gate[i], w_ple_proj[i])
        hs = hs + ple(hs, p_sample[i], g_ple[i], w_ple_gate[i], w_ple_proj[i])
    y_prompt = rmsnorm(hp, g_final)
    y_sample = rmsnorm(hs, g_final)
    return (y_prompt, y_sample,
            jnp.stack(hgrn_p), jnp.stack(s5r_p), jnp.stack(s5i_p),
            jnp.stack(hgrn_s), jnp.stack(s5r_s), jnp.stack(s5i_s))
```

```python
import functools
import math

import jax
import jax.numpy as jnp
from jax import lax
from jax.experimental import pallas as pl
from jax.experimental.pallas import tpu as pltpu

F32 = jnp.float32
BF16 = jnp.bfloat16
HIGHEST = lax.Precision.HIGHEST

EPS = 1e-6
SWIGLU_LIMIT = 7.0
SWIGLU_ALPHA = 1.702
TOP_K = 4

LANES = 128
SUBLANES = 8
VMEM_BYTES_V7X = 64 * 1024 * 1024
VMEM_LIMIT = VMEM_BYTES_V7X - 6 * 1024 * 1024

HEAD_DIM = 128
HGRN_CHUNK = 128
HGRN_SUB = 16
HGRN_STEP_ROWS = 256
S5_PACK = 8
S5_TT = 128

MOE_SUB = 256
MOE_ITEM_SUBS = 6
MOE_ITEM_ROWS = MOE_SUB * MOE_ITEM_SUBS


def _params(sem, vmem=VMEM_LIMIT):
    return pltpu.CompilerParams(dimension_semantics=sem, vmem_limit_bytes=vmem)


def _dot_nt(a, b):
    return lax.dot_general(a, b, (((1,), (1,)), ((), ())), preferred_element_type=F32)


def _dot_tn(a, b):
    return lax.dot_general(a, b, (((0,), (0,)), ((), ())), preferred_element_type=F32)


def _inproj_kernel(x_ref, g_ref, w_ref, z_ref, n_sc):
    @pl.when(pl.program_id(1) == 0)
    def _():
        x = x_ref[...]
        ms = jnp.mean(x * x, axis=-1, keepdims=True)
        n_sc[...] = (x * lax.rsqrt(ms + EPS) * g_ref[...]).astype(BF16)

    z_ref[...] = jnp.dot(n_sc[...], w_ref[...], preferred_element_type=F32)


def _in_proj(x, g, w_bf, tm=640, tn=512):
    n, d = x.shape
    cols = w_bf.shape[1]
    return pl.pallas_call(
        _inproj_kernel,
        out_shape=jax.ShapeDtypeStruct((n, cols), F32),
        grid=(n // tm, cols // tn),
        in_specs=[pl.BlockSpec((tm, d), lambda i, j: (i, 0)),
                  pl.BlockSpec((1, d), lambda i, j: (0, 0)),
                  pl.BlockSpec((d, tn), lambda i, j: (0, j))],
        out_specs=pl.BlockSpec((tm, tn), lambda i, j: (i, j)),
        scratch_shapes=[pltpu.VMEM((tm, d), BF16)],
        compiler_params=_params(("parallel", "arbitrary")),
        name="in_proj",
    )(x, g.reshape(1, d), w_bf)


def _gates(fz, lb):
    f = lb + (1.0 - lb) * jax.nn.sigmoid(fz)
    k = (1.0 - lb) * jax.nn.sigmoid(-fz)
    return f, k


def _onorm_gate(o, gon, gz):
    ms = jnp.mean(o * o, axis=-1, keepdims=True)
    return o * lax.rsqrt(ms + EPS) * gon * (gz * jax.nn.sigmoid(gz))


def _hgrn_prompt_kernel(q_ref, f_ref, v_ref, gz_ref, lb_ref, gon_ref, o_ref, s_ref, st_sc):
    c_idx = pl.program_id(2)
    C, SUB = HGRN_CHUNK, HGRN_SUB

    @pl.when(c_idx == 0)
    def _():
        st_sc[...] = jnp.zeros_like(st_sc)

    lb = lb_ref[...]
    gon = gon_ref[...]
    r = lax.broadcasted_iota(jnp.int32, (C, C), 0)
    c = lax.broadcasted_iota(jnp.int32, (C, C), 1)
    tri = (r >= c).astype(F32)
    rows = lax.broadcasted_iota(jnp.int32, (SUB, HEAD_DIM), 0)
    rfull = lax.broadcasted_iota(jnp.int32, (C, HEAD_DIM), 0)
    levels = []
    m = SUB
    while m < C:
        pair = jnp.logical_and((r // (2 * m)) == (c // (2 * m)),
                               jnp.logical_and((r % (2 * m)) >= m, (c % (2 * m)) < m))
        levels.append((m, pair, (rfull % (2 * m)) >= m))
        m *= 2

    for ci in range(HGRN_STEP_ROWS // C):
        sl = slice(ci * C, (ci + 1) * C)
        q, fz, v, gz = q_ref[sl, :], f_ref[sl, :], v_ref[sl, :], gz_ref[sl, :]
        f, k = _gates(fz, lb)
        b = jnp.dot(tri, jnp.log(f), precision=HIGHEST, preferred_element_type=F32)
        st = st_sc[...]
        o = _dot_nt(q * jnp.exp(b), st)
        att = jnp.zeros((C, C), F32)
        for m, pair, right in levels:
            bs = jnp.concatenate([jnp.broadcast_to(b[r0 + m - 1:r0 + m], (2 * m, HEAD_DIM))
                                  for r0 in range(0, C, 2 * m)], axis=0)
            e = jnp.exp(jnp.where(right, b - bs, bs - b))
            att = att + jnp.where(pair, _dot_nt(q * e, k * e), 0.0)
        o = o + jnp.dot(att, v, preferred_element_type=F32)
        parts = []
        for i in range(C // SUB):
            r0 = i * SUB
            qi, bi, ki, vi = q[r0:r0 + SUB], b[r0:r0 + SUB], k[r0:r0 + SUB], v[r0:r0 + SUB]
            oi = o[r0:r0 + SUB]
            for s in range(SUB):
                e = jnp.exp(jnp.where(rows >= s, bi - bi[s:s + 1], -jnp.inf))
                a_s = jnp.sum(qi * e * ki[s:s + 1], axis=-1, keepdims=True)
                oi = oi + a_s * vi[s:s + 1]
            parts.append(oi)
        o = jnp.concatenate(parts, axis=0)
        bl = b[C - 1:C]
        st = st * jnp.exp(bl) + _dot_tn(v, k * jnp.exp(bl - b))
        st_sc[...] = st
        o_ref[sl, :] = _onorm_gate(o, gon, gz)

    @pl.when(c_idx == pl.num_programs(2) - 1)
    def _():
        s_ref[...] = st_sc[...].T


def _hgrn_prompt(z, bsz, T, lb, g_onorm, heads):
    R = HGRN_STEP_ROWS
    col = lambda off: pl.BlockSpec((R, HEAD_DIM), lambda b, h, c: (b * (T // R) + c, off + h))
    vec = pl.BlockSpec((1, HEAD_DIM), lambda b, h, c: (0, h))
    return pl.pallas_call(
        _hgrn_prompt_kernel,
        out_shape=(jax.ShapeDtypeStruct((bsz, T, heads * HEAD_DIM), F32),
                   jax.ShapeDtypeStruct((bsz, heads, HEAD_DIM, HEAD_DIM), F32)),
        grid=(bsz, heads, T // R),
        in_specs=[col(0), col(heads), col(2 * heads), col(3 * heads), vec, vec],
        out_specs=(pl.BlockSpec((None, R, HEAD_DIM), lambda b, h, c: (b, c, h)),
                   pl.BlockSpec((None, None, HEAD_DIM, HEAD_DIM), lambda b, h, c: (b, h, 0, 0))),
        scratch_shapes=[pltpu.VMEM((HEAD_DIM, HEAD_DIM), F32)],
        compiler_params=_params(("parallel", "parallel", "arbitrary")),
        name="hgrn_prompt",
    )(z, z, z, z, lb.reshape(1, -1), g_onorm.reshape(1, -1))


def _hgrn_sample_kernel(qT_ref, fT_ref, v_ref, gz_ref, lbT_ref, gon_ref, s0_ref, o_ref, s_ref):
    lbT = lbT_ref[...]
    fT, kT = _gates(fT_ref[...], lbT)
    qT = qT_ref[...]
    v = v_ref[...]
    for bb in range(v.shape[0]):
        s_new = fT[:, bb:bb + 1] * s0_ref[bb] + kT[:, bb:bb + 1] * v[bb:bb + 1, :]
        s_ref[bb] = s_new
        o_ref[bb:bb + 1, :] = jnp.sum(qT[:, bb:bb + 1] * s_new, axis=0, keepdims=True)
    o_ref[...] = _onorm_gate(o_ref[...], gon_ref[...], gz_ref[...])


def _hgrn_sample(z_s, s0, lb, g_onorm, heads):
    nb = bt = z_s.shape[0]
    z_sT = z_s[:, :2 * heads * HEAD_DIM].T
    lbT = jnp.broadcast_to(lb[:, None], (heads * HEAD_DIM, bt))
    return pl.pallas_call(
        _hgrn_sample_kernel,
        out_shape=(jax.ShapeDtypeStruct((nb, heads * HEAD_DIM), F32),
                   jax.ShapeDtypeStruct(s0.shape, F32)),
        grid=(heads, nb // bt),
        in_specs=[pl.BlockSpec((HEAD_DIM, bt), lambda h, b: (h, b)),
                  pl.BlockSpec((HEAD_DIM, bt), lambda h, b: (heads + h, b)),
                  pl.BlockSpec((bt, HEAD_DIM), lambda h, b: (b, 2 * heads + h)),
                  pl.BlockSpec((bt, HEAD_DIM), lambda h, b: (b, 3 * heads + h)),
                  pl.BlockSpec((HEAD_DIM, bt), lambda h, b: (h, 0)),
                  pl.BlockSpec((1, HEAD_DIM), lambda h, b: (0, h)),
                  pl.BlockSpec((bt, None, HEAD_DIM, HEAD_DIM), lambda h, b: (b, h, 0, 0))],
        out_specs=(pl.BlockSpec((bt, HEAD_DIM), lambda h, b: (b, h)),
                   pl.BlockSpec((bt, None, HEAD_DIM, HEAD_DIM), lambda h, b: (b, h, 0, 0))),
        compiler_params=_params(("parallel", "parallel")),
        name="hgrn_sample",
    )(z_sT, z_sT, z_s, z_s, lbT, g_onorm.reshape(1, -1), s0)


def _s5_tables(a_re, a_im, log_dt, b_re, b_im, c_re, c_im, d_skip, w1, b1, w2, b2, tt):
    G, P = a_re.shape
    HG = d_skip.shape[-1]
    SG = G // S5_PACK
    dt = jnp.exp(log_dt)
    ar, ai = a_re * dt, a_im * dt
    mag = jnp.exp(ar)
    abar_re, abar_im = mag * jnp.cos(ai), mag * jnp.sin(ai)
    den = a_re * a_re + a_im * a_im
    nr = abar_re - 1.0
    cr = (nr * a_re + abar_im * a_im) / den
    ci = (abar_im * a_re - nr * a_im) / den
    bb_re = cr[..., None] * b_re - ci[..., None] * b_im
    bb_im = cr[..., None] * b_im + ci[..., None] * b_re
    eye = jnp.eye(S5_PACK, dtype=F32)

    def bdiag_in(bb):
        t = jnp.einsum('sgph,gk->sghkp', bb.reshape(SG, S5_PACK, P, HG), eye)
        return t.reshape(SG, S5_PACK * HG, S5_PACK * P).astype(BF16)

    def bdiag_out(cc):
        t = jnp.einsum('sghp,gk->sgpkh', cc.reshape(SG, S5_PACK, HG, P), eye)
        return t.reshape(SG, S5_PACK * P, S5_PACK * HG).astype(BF16)

    def bdiag_sq(w):
        t = jnp.einsum('sghk,gj->sghjk', w.reshape(SG, S5_PACK, HG, HG), eye)
        return t.reshape(SG, S5_PACK * HG, S5_PACK * HG).astype(BF16)

    def powers(steps):
        st = steps.astype(F32)[:, None, None]
        pm = jnp.exp(st * ar)
        return ((pm * jnp.cos(st * ai)).reshape(len(steps), G * P),
                (pm * jnp.sin(st * ai)).reshape(len(steps), G * P))

    nk = int(math.log2(tt))
    p2 = powers(2 ** jnp.arange(SUBLANES if nk <= SUBLANES else nk))
    pw = powers(jnp.arange(1, tt + 1))
    return dict(
        abar=(abar_re.reshape(1, G * P), abar_im.reshape(1, G * P)),
        bb=(bdiag_in(bb_re), bdiag_in(bb_im)), cc=(bdiag_out(c_re), bdiag_out(c_im)),
        d=d_skip.reshape(1, G * HG), w1=bdiag_sq(w1), b1=b1.reshape(1, G * HG),
        w2=bdiag_sq(w2), b2=b2.reshape(1, G * HG), p2=p2, pw=pw)


def _s5_readout(u, xr, xi, cre_ref, cim_ref, d_ref, w1_ref, b1_ref, w2_ref, b2_ref):
    y = (jnp.dot(xr.astype(BF16), cre_ref[...], preferred_element_type=F32)
         - jnp.dot(xi.astype(BF16), cim_ref[...], preferred_element_type=F32) + d_ref[...] * u)
    yg = jax.nn.gelu(y).astype(BF16)
    return ((jnp.dot(yg, w1_ref[...], preferred_element_type=F32) + b1_ref[...])
            * jax.nn.sigmoid(jnp.dot(yg, w2_ref[...], preferred_element_type=F32) + b2_ref[...]))


def _shift_rows(x, s, row_ids):
    if s % SUBLANES == 0:
        return jnp.concatenate([jnp.zeros((s, x.shape[1]), x.dtype), x[:-s]], axis=0)
    return jnp.where(row_ids >= s, pltpu.roll(x, s, 0), 0.0)


def _s5_prompt_kernel(u_ref, bre_ref, bim_ref, cre_ref, cim_ref, d_ref, w1_ref, b1_ref, w2_ref, b2_ref,
                      p2re_ref, p2im_ref, pwre_ref, pwim_ref, o_ref, xr_ref, xi_ref, car_re, car_im):
    t_idx = pl.program_id(2)
    tt, width = pwre_ref.shape

    @pl.when(t_idx == 0)
    def _():
        car_re[...] = jnp.zeros_like(car_re)
        car_im[...] = jnp.zeros_like(car_im)

    u = u_ref[...]
    ub = u.astype(BF16)
    xr = jnp.dot(ub, bre_ref[...], preferred_element_type=F32)
    xi = jnp.dot(ub, bim_ref[...], preferred_element_type=F32)
    row_ids = lax.broadcasted_iota(jnp.int32, (tt, width), 0)
    for kk in range(int(math.log2(tt))):
        s = 1 << kk
        pr, pi = p2re_ref[kk:kk + 1, :], p2im_ref[kk:kk + 1, :]
        sr, si = _shift_rows(xr, s, row_ids), _shift_rows(xi, s, row_ids)
        xr, xi = xr + pr * sr - pi * si, xi + pr * si + pi * sr
    cr, ci = car_re[...], car_im[...]
    pr, pi = pwre_ref[...], pwim_ref[...]
    xr, xi = xr + pr * cr - pi * ci, xi + pr * ci + pi * cr
    car_re[...] = xr[tt - 1:tt]
    car_im[...] = xi[tt - 1:tt]
    o_ref[...] = _s5_readout(u, xr, xi, cre_ref, cim_ref, d_ref, w1_ref, b1_ref, w2_ref, b2_ref)

    @pl.when(t_idx == pl.num_programs(2) - 1)
    def _():
        xr_ref[...] = xr[tt - 1:tt]
        xi_ref[...] = xi[tt - 1:tt]


def _s5_prompt(z, bsz, T, tb, u_col0):
    sg_n, uw, sw = tb['bb'][0].shape
    tt = tb['pw'][0].shape[0]
    nk = tb['p2'][0].shape[0]
    per_sg = lambda shp: pl.BlockSpec((None,) + shp, lambda g, b, t: (g, 0, 0))
    vec = pl.BlockSpec((1, uw), lambda g, b, t: (0, g))
    o_b, xr, xi = pl.pallas_call(
        _s5_prompt_kernel,
        out_shape=(jax.ShapeDtypeStruct((bsz, T, sg_n * uw), F32),
                   jax.ShapeDtypeStruct((bsz, 1, sg_n * sw), F32),
                   jax.ShapeDtypeStruct((bsz, 1, sg_n * sw), F32)),
        grid=(sg_n, bsz, T // tt),
        in_specs=[pl.BlockSpec((tt, uw), lambda g, b, t: (b * (T // tt) + t, u_col0 + g)),
                  per_sg((uw, sw)), per_sg((uw, sw)), per_sg((sw, uw)), per_sg((sw, uw)),
                  vec, per_sg((uw, uw)), vec, per_sg((uw, uw)), vec,
                  pl.BlockSpec((nk, sw), lambda g, b, t: (0, g)),
                  pl.BlockSpec((nk, sw), lambda g, b, t: (0, g)),
                  pl.BlockSpec((tt, sw), lambda g, b, t: (0, g)),
                  pl.BlockSpec((tt, sw), lambda g, b, t: (0, g))],
        out_specs=(pl.BlockSpec((None, tt, uw), lambda g, b, t: (b, t, g)),
                   pl.BlockSpec((None, 1, sw), lambda g, b, t: (b, 0, g)),
                   pl.BlockSpec((None, 1, sw), lambda g, b, t: (b, 0, g))),
        scratch_shapes=[pltpu.VMEM((1, sw), F32), pltpu.VMEM((1, sw), F32)],
        compiler_params=_params(("parallel", "parallel", "arbitrary")),
        name="s5_prompt",
    )(z, tb['bb'][0], tb['bb'][1], tb['cc'][0], tb['cc'][1], tb['d'], tb['w1'], tb['b1'], tb['w2'],
      tb['b2'], tb['p2'][0], tb['p2'][1], tb['pw'][0], tb['pw'][1])
    return o_b, xr, xi


def _s5_sample_kernel(u_ref, x0r_ref, x0i_ref, are_ref, aim_ref, bre_ref, bim_ref, cre_ref, cim_ref, d_ref,
                      w1_ref, b1_ref, w2_ref, b2_ref, o_ref, xr_ref, xi_ref):
    u = u_ref[...]
    ub = u.astype(BF16)
    ar, ai = are_ref[...], aim_ref[...]
    x0r, x0i = x0r_ref[...], x0i_ref[...]
    xr = jnp.dot(ub, bre_ref[...], preferred_element_type=F32) + ar * x0r - ai * x0i
    xi = jnp.dot(ub, bim_ref[...], preferred_element_type=F32) + ar * x0i + ai * x0r
    xr_ref[...] = xr
    xi_ref[...] = xi
    o_ref[...] = _s5_readout(u, xr, xi, cre_ref, cim_ref, d_ref, w1_ref, b1_ref, w2_ref, b2_ref)


def _s5_sample(z_s, x0r, x0i, tb, u_col0):
    nb = z_s.shape[0]
    sg_n, uw, sw = tb['bb'][0].shape
    per_sg = lambda shp: pl.BlockSpec((None,) + shp, lambda g: (g, 0, 0))
    vec = pl.BlockSpec((1, uw), lambda g: (0, g))
    st = pl.BlockSpec((nb, sw), lambda g: (0, g))
    svec = pl.BlockSpec((1, sw), lambda g: (0, g))
    return pl.pallas_call(
        _s5_sample_kernel,
        out_shape=(jax.ShapeDtypeStruct((nb, sg_n * uw), F32),
                   jax.ShapeDtypeStruct((nb, sg_n * sw), F32),
                   jax.ShapeDtypeStruct((nb, sg_n * sw), F32)),
        grid=(sg_n,),
        in_specs=[pl.BlockSpec((nb, uw), lambda g: (0, u_col0 + g)), st, st, svec, svec,
                  per_sg((uw, sw)), per_sg((uw, sw)), per_sg((sw, uw)), per_sg((sw, uw)),
                  vec, per_sg((uw, uw)), vec, per_sg((uw, uw)), vec],
        out_specs=(pl.BlockSpec((nb, uw), lambda g: (0, g)), st, st),
        compiler_params=_params(("parallel",)),
        name="s5_sample",
    )(z_s, x0r, x0i, tb['abar'][0], tb['abar'][1], tb['bb'][0], tb['bb'][1], tb['cc'][0], tb['cc'][1],
      tb['d'], tb['w1'], tb['b1'], tb['w2'], tb['b2'])


def _outproj_kernel(oa_ref, ob_ref, x_ref, w_ref, g_ref, wr_ref, br_ref,
                    h1_ref, tok_ref, ti_ref, tw_ref, cat_sc, *, n_experts):
    j = pl.program_id(1)
    nj = pl.num_programs(1)
    tn = w_ref.shape[1]
    half = oa_ref.shape[1]

    @pl.when(j == 0)
    def _():
        cat_sc[:, :half] = oa_ref[...].astype(BF16)
        cat_sc[:, half:] = ob_ref[...].astype(BF16)

    h = x_ref[...] + jnp.dot(cat_sc[...], w_ref[...], preferred_element_type=F32)
    for jj in range(h1_ref.shape[1] // tn):
        @pl.when(j == jj)
        def _():
            h1_ref[:, jj * tn:(jj + 1) * tn] = h

    @pl.when(j == nj - 1)
    def _():
        h1 = h1_ref[...]
        ms = jnp.mean(h1 * h1, axis=-1, keepdims=True)
        tok = h1 * lax.rsqrt(ms + EPS) * g_ref[...]
        tok_ref[...] = tok.astype(BF16)
        logits = jnp.dot(tok, wr_ref[...], precision=HIGHEST, preferred_element_type=F32) + br_ref[...]
        lane = lax.broadcasted_iota(jnp.int32, logits.shape, 1).astype(F32)
        l = jnp.where(lane < n_experts, logits, -jnp.inf)
        vals, idxs = [], []
        for _ in range(TOP_K):
            m = jnp.max(l, axis=-1, keepdims=True)
            i = jnp.min(jnp.where(l == m, lane, float(LANES)), axis=-1, keepdims=True)
            vals.append(m)
            idxs.append(i)
            l = jnp.where(lane == i, -jnp.inf, l)
        es = [jnp.exp(vv - vals[0]) for vv in vals]
        tot = es[0] + es[1] + es[2] + es[3]
        ti = jnp.zeros(logits.shape, jnp.int32)
        tw = jnp.zeros(logits.shape, F32)
        for r in range(TOP_K):
            ti = jnp.where(lane == r, idxs[r].astype(jnp.int32), ti)
            tw = jnp.where(lane == r, es[r] / tot, tw)
        ti_ref[...] = ti
        tw_ref[...] = tw


def _out_proj(o_a, o_b, x, w_bf, g_ffn, w_router, b_router, tm=320, tn=512):
    n, d = x.shape
    half = o_a.shape[1]
    n_experts = w_router.shape[1]
    wr = jnp.zeros((d, LANES), F32).at[:, :n_experts].set(w_router)
    br = jnp.zeros((1, LANES), F32).at[0, :n_experts].set(b_router)
    row = lambda w: pl.BlockSpec((tm, w), lambda i, j: (i, 0))
    return pl.pallas_call(
        functools.partial(_outproj_kernel, n_experts=n_experts),
        out_shape=(jax.ShapeDtypeStruct((n, d), F32), jax.ShapeDtypeStruct((n, d), BF16),
                   jax.ShapeDtypeStruct((n, LANES), jnp.int32), jax.ShapeDtypeStruct((n, LANES), F32)),
        grid=(n // tm, d // tn),
        in_specs=[row(half), row(half),
                  pl.BlockSpec((tm, tn), lambda i, j: (i, j)),
                  pl.BlockSpec((d, tn), lambda i, j: (0, j)),
                  pl.BlockSpec((1, d), lambda i, j: (0, 0)),
                  pl.BlockSpec((d, LANES), lambda i, j: (0, 0)),
                  pl.BlockSpec((1, LANES), lambda i, j: (0, 0))],
        out_specs=(row(d), row(d), row(LANES), row(LANES)),
        scratch_shapes=[pltpu.VMEM((tm, d), BF16)],
        compiler_params=_params(("parallel", "arbitrary")),
        name="out_proj_router",
    )(o_a, o_b, x, w_bf, g_ffn.reshape(1, d), wr, br)


def _moe_plan(top_i, n_experts):
    n_assign = top_i.size
    flat_e = top_i.reshape(-1)
    onehot = (flat_e[:, None] == jnp.arange(n_experts, dtype=jnp.int32)[None, :]).astype(jnp.int32)
    counts = onehot.sum(axis=0)
    rank = jnp.take_along_axis(jnp.cumsum(onehot, axis=0) - onehot, flat_e[:, None], axis=1)[:, 0]
    nsub_e = (counts + MOE_SUB - 1) // MOE_SUB
    group_start = (jnp.cumsum(nsub_e) - nsub_e) * MOE_SUB
    nitems_e = (nsub_e + MOE_ITEM_SUBS - 1) // MOE_ITEM_SUBS
    item_end = jnp.cumsum(nitems_e)
    item_start = item_end - nitems_e
    n_items = n_assign // MOE_ITEM_ROWS + 1 + n_experts
    m_pad = n_assign + n_experts * MOE_SUB
    w = jnp.arange(n_items, dtype=jnp.int32)
    total = item_end[-1]
    w_eff = jnp.minimum(w, total - 1)
    item_e = jnp.minimum(jnp.searchsorted(item_end, w_eff, side='right'), n_experts - 1).astype(jnp.int32)
    local = w_eff - item_start[item_e]
    item_nsub = jnp.where(w < total, jnp.clip(nsub_e[item_e] - local * MOE_ITEM_SUBS, 0, MOE_ITEM_SUBS), 0)
    item_row0 = group_start[item_e] + local * MOE_ITEM_ROWS
    item_slot = w_eff
    cpos = group_start[flat_e] + rank
    ipos = (item_start[flat_e] + rank // MOE_ITEM_ROWS) * MOE_ITEM_ROWS + rank % MOE_ITEM_ROWS
    src_token = jnp.zeros((m_pad,), jnp.int32).at[cpos].set(
        jnp.arange(n_assign, dtype=jnp.int32) // top_i.shape[1])
    meta = jnp.stack([item_e, item_row0, item_nsub, item_slot]).astype(jnp.int32)
    return meta, src_token, ipos, n_items


def _moe_up_kernel(meta_ref, xs_hbm, wg_ref, wu_ref, bg_ref, bu_ref, act_ref, lhs_sc, wg_sc, wu_sc, sems):
    w, j = pl.program_id(0), pl.program_id(1)
    row0 = meta_ref[1, w]
    nsub = meta_ref[2, w]

    def lhs_copy(s):
        src = xs_hbm.at[pl.ds(pl.multiple_of(row0 + s * MOE_SUB, MOE_SUB), MOE_SUB), :]
        return pltpu.make_async_copy(src, lhs_sc.at[pl.ds(s * MOE_SUB, MOE_SUB), :], sems.at[s])

    @pl.when(j == 0)
    def _():
        for s in range(MOE_ITEM_SUBS):
            @pl.when(s < nsub)
            def _():
                lhs_copy(s).start()
        for s in range(MOE_ITEM_SUBS):
            @pl.when(s < nsub)
            def _():
                lhs_copy(s).wait()

    @pl.when(nsub > 0)
    def _():
        wg_sc[...] = wg_ref[...].astype(BF16)
        wu_sc[...] = wu_ref[...].astype(BF16)

    for s in range(MOE_ITEM_SUBS):
        rows = slice(s * MOE_SUB, (s + 1) * MOE_SUB)

        @pl.when(s < nsub)
        def _():
            x = lhs_sc[rows, :]
            gate = jnp.dot(x, wg_sc[...], preferred_element_type=F32) + bg_ref[...]
            up = jnp.dot(x, wu_sc[...], preferred_element_type=F32) + bu_ref[...]
            gate = jnp.minimum(gate, SWIGLU_LIMIT)
            up = jnp.clip(up, -SWIGLU_LIMIT, SWIGLU_LIMIT)
            act_ref[rows, :] = ((up + 1.0) * gate * jax.nn.sigmoid(SWIGLU_ALPHA * gate)).astype(BF16)

        @pl.when(jnp.logical_and(s >= nsub, nsub > 0))
        def _():
            act_ref[rows, :] = jnp.zeros((MOE_SUB, act_ref.shape[1]), BF16)


def _moe_up(meta, xs, w_gate_up, b_gate_up, n_items, tn=256):
    n_experts, d, two_ff = w_gate_up.shape
    d_ff = two_ff // 2
    nj = d_ff // tn
    jj = lambda w, j, m: jnp.where(m[2, w] > 0, j, nj - 1)
    grid_spec = pltpu.PrefetchScalarGridSpec(
        num_scalar_prefetch=1,
        grid=(n_items, nj),
        in_specs=[pl.BlockSpec(memory_space=pl.ANY),
                  pl.BlockSpec((None, d, tn), lambda w, j, m: (m[0, w], 0, jj(w, j, m))),
                  pl.BlockSpec((None, d, tn), lambda w, j, m: (m[0, w], 0, nj + jj(w, j, m))),
                  pl.BlockSpec((None, 1, tn), lambda w, j, m: (m[0, w], 0, jj(w, j, m))),
                  pl.BlockSpec((None, 1, tn), lambda w, j, m: (m[0, w], 0, nj + jj(w, j, m)))],
        out_specs=pl.BlockSpec((None, MOE_ITEM_ROWS, tn), lambda w, j, m: (m[3, w], 0, jj(w, j, m))),
        scratch_shapes=[pltpu.VMEM((MOE_ITEM_ROWS, d), BF16), pltpu.VMEM((d, tn), BF16),
                        pltpu.VMEM((d, tn), BF16), pltpu.SemaphoreType.DMA((MOE_ITEM_SUBS,))])
    bgu = b_gate_up.reshape(n_experts, 1, two_ff)
    return pl.pallas_call(
        _moe_up_kernel,
        out_shape=jax.ShapeDtypeStruct((n_items, MOE_ITEM_ROWS, d_ff), BF16),
        grid_spec=grid_spec,
        compiler_params=_params(("arbitrary", "arbitrary")),
        name="moe_gate_up",
    )(meta, xs, w_gate_up, w_gate_up, bgu, bgu)


def _moe_down_kernel(meta_ref, act_ref, wd_ref, bd_ref, y_ref, wd_sc):
    w = pl.program_id(0)
    nsub = meta_ref[2, w]

    @pl.when(nsub > 0)
    def _():
        wd_sc[...] = wd_ref[...].astype(BF16)

    for s in range(MOE_ITEM_SUBS):
        rows = slice(s * MOE_SUB, (s + 1) * MOE_SUB)

        @pl.when(s < nsub)
        def _():
            y_ref[rows, :] = jnp.dot(act_ref[rows, :], wd_sc[...], preferred_element_type=F32) + bd_ref[...]

        @pl.when(jnp.logical_and(s >= nsub, nsub > 0))
        def _():
            y_ref[rows, :] = jnp.zeros((MOE_SUB, y_ref.shape[1]), F32)


def _moe_down(meta, act, w_down, b_down, tn=256):
    n_experts, d_ff, d = w_down.shape
    n_items = act.shape[0]
    nj = d // tn
    jj = lambda w, j, m: jnp.where(m[2, w] > 0, j, nj - 1)
    grid_spec = pltpu.PrefetchScalarGridSpec(
        num_scalar_prefetch=1,
        grid=(n_items, nj),
        in_specs=[pl.BlockSpec((None, MOE_ITEM_ROWS, d_ff), lambda w, j, m: (m[3, w], 0, 0)),
                  pl.BlockSpec((None, d_ff, tn), lambda w, j, m: (m[0, w], 0, jj(w, j, m))),
                  pl.BlockSpec((None, 1, tn), lambda w, j, m: (m[0, w], 0, jj(w, j, m)))],
        out_specs=pl.BlockSpec((None, MOE_ITEM_ROWS, tn), lambda w, j, m: (m[3, w], 0, jj(w, j, m))),
        scratch_shapes=[pltpu.VMEM((d_ff, tn), BF16)])
    return pl.pallas_call(
        _moe_down_kernel,
        out_shape=jax.ShapeDtypeStruct((n_items, MOE_ITEM_ROWS, d), F32),
        grid_spec=grid_spec,
        compiler_params=_params(("arbitrary", "arbitrary")),
        name="moe_down",
    )(meta, act, w_down, b_down.reshape(n_experts, 1, d))


def _ple_norm_kernel(h1_ref, ffn_ref, g_ref, h2_ref, n_ref):
    h2 = h1_ref[...] + ffn_ref[...]
    ms = jnp.mean(h2 * h2, axis=-1, keepdims=True)
    h2_ref[...] = h2
    n_ref[...] = (h2 * lax.rsqrt(ms + EPS) * g_ref[...]).astype(BF16)


def _ple_norm(h1, ffn, g_ple, tm=320):
    n, d = h1.shape
    row = pl.BlockSpec((tm, d), lambda i: (i, 0))
    return pl.pallas_call(
        _ple_norm_kernel,
        out_shape=(jax.ShapeDtypeStruct((n, d), F32), jax.ShapeDtypeStruct((n, d), BF16)),
        grid=(n // tm,),
        in_specs=[row, row, pl.BlockSpec((1, d), lambda i: (0, 0))],
        out_specs=(row, row),
        compiler_params=_params(("parallel",)),
        name="ple_norm",
    )(h1, ffn, g_ple.reshape(1, d))


def _ple_kernel(n_ref, p_ref, h2_ref, wg_ref, wp_ref, gf_ref, y_ref):
    j = pl.program_id(1)
    tn = wg_ref.shape[1]
    gate = jax.nn.sigmoid(jnp.dot(n_ref[...], wg_ref[...], preferred_element_type=F32))
    emb = jnp.dot(p_ref[...].astype(BF16), wp_ref[...], preferred_element_type=F32)
    h3 = h2_ref[...] + gate * emb
    for jj in range(y_ref.shape[1] // tn):
        @pl.when(j == jj)
        def _():
            y_ref[:, jj * tn:(jj + 1) * tn] = h3

    @pl.when(j == pl.num_programs(1) - 1)
    def _():
        h = y_ref[...]
        ms = jnp.mean(h * h, axis=-1, keepdims=True)
        y_ref[...] = h * lax.rsqrt(ms + EPS) * gf_ref[...]


def _ple(n3, p, h2, wg_bf, wp_bf, g_final, tm=320, tn=512):
    n, d = h2.shape
    pd = p.shape[1]
    return pl.pallas_call(
        _ple_kernel,
        out_shape=jax.ShapeDtypeStruct((n, d), F32),
        grid=(n // tm, d // tn),
        in_specs=[pl.BlockSpec((tm, d), lambda i, j: (i, 0)),
                  pl.BlockSpec((tm, pd), lambda i, j: (i, 0)),
                  pl.BlockSpec((tm, tn), lambda i, j: (i, j)),
                  pl.BlockSpec((d, tn), lambda i, j: (0, j)),
                  pl.BlockSpec((pd, tn), lambda i, j: (0, j)),
                  pl.BlockSpec((1, d), lambda i, j: (0, 0))],
        out_specs=pl.BlockSpec((tm, d), lambda i, j: (i, 0)),
        compiler_params=_params(("parallel", "arbitrary")),
        name="ple_final",
    )(n3, p, h2, wg_bf, wp_bf, g_final.reshape(1, d))


def kernel(x_prompt, x_sample, state_hgrn, state_s5_re, state_s5_im, p_prompt, p_sample, g_mix, w_in, hgrn_gamma, g_onorm, s5_a_re, s5_a_im, s5_log_dt, s5_b_re, s5_b_im, s5_c_re, s5_c_im, s5_d, glu_w1, glu_b1, glu_w2, glu_b2, w_out, g_ffn, w_router, b_router, w_gate_up, b_gate_up, w_down, b_down, g_ple, w_ple_gate, w_ple_proj, g_final):
    assert w_in.shape[0] == 1, "single-layer trunk"
    bsz, T, d = x_prompt.shape
    nb = x_sample.shape[0]
    n_prompt = bsz * T
    heads = hgrn_gamma.shape[1] // HEAD_DIM
    G, P = s5_a_re.shape[1:]
    n_experts = w_router.shape[-1]
    u_col0 = 4 * heads

    lb = jnp.cumsum(jax.nn.softmax(hgrn_gamma.astype(F32), axis=0), axis=0)[0]
    h = jnp.concatenate([x_prompt.reshape(n_prompt, d), x_sample.reshape(nb, d)], axis=0)
    z = _in_proj(h, g_mix[0], w_in[0].astype(BF16))
    z_s = z[n_prompt:]
    oa_p, s_p = _hgrn_prompt(z, bsz, T, lb, g_onorm[0], heads)
    oa_s, s_s = _hgrn_sample(z_s, state_hgrn[0], lb, g_onorm[0], heads)
    tb = _s5_tables(s5_a_re[0], s5_a_im[0], s5_log_dt[0], s5_b_re[0], s5_b_im[0], s5_c_re[0], s5_c_im[0],
                    s5_d[0], glu_w1[0], glu_b1[0], glu_w2[0], glu_b2[0], S5_TT)
    ob_p, xr_p, xi_p = _s5_prompt(z, bsz, T, tb, u_col0)
    ob_s, xr_s, xi_s = _s5_sample(z_s, state_s5_re[0].reshape(nb, G * P), state_s5_im[0].reshape(nb, G * P),
                                  tb, u_col0)
    o_a = jnp.concatenate([oa_p.reshape(n_prompt, -1), oa_s], axis=0)
    o_b = jnp.concatenate([ob_p.reshape(n_prompt, -1), ob_s], axis=0)
    h1, tok, ti, tw = _out_proj(o_a, o_b, h, w_out[0].astype(BF16), g_ffn[0], w_router[0], b_router[0])
    top_i, top_w = ti[:, :TOP_K], tw[:, :TOP_K]
    meta, src_token, ipos, n_items = _moe_plan(top_i, n_experts)
    xs = jnp.take(tok, src_token, axis=0)
    act = _moe_up(meta, xs, w_gate_up[0], b_gate_up[0], n_items)
    y_items = _moe_down(meta, act, w_down[0], b_down[0])
    y_sel = jnp.take(y_items.reshape(-1, d), ipos, axis=0).reshape(-1, TOP_K, d)
    ffn = jnp.sum(top_w[:, :, None] * y_sel, axis=1)
    h2, n3 = _ple_norm(h1, ffn, g_ple[0])
    p_all = jnp.concatenate([p_prompt[0].reshape(n_prompt, -1), p_sample[0].reshape(nb, -1)], axis=0)
    y = _ple(n3, p_all, h2, w_ple_gate[0].astype(BF16), w_ple_proj[0].astype(BF16), g_final)
    return (y[:n_prompt].reshape(bsz, T, d), y[n_prompt:].reshape(nb, 1, d),
            s_p[None], xr_p.reshape(1, bsz, G, P), xi_p.reshape(1, bsz, G, P),
            s_s[None], xr_s.reshape(1, nb, G, P), xi_s.reshape(1, nb, G, P))
```

```python
import functools
import math

import jax
import jax.numpy as jnp
from jax import lax
from jax.experimental import pallas as pl
from jax.experimental.pallas import tpu as pltpu

F32 = jnp.float32
BF16 = jnp.bfloat16
HIGHEST = lax.Precision.HIGHEST

EPS = 1e-6
SWIGLU_LIMIT = 7.0
SWIGLU_ALPHA = 1.702
TOP_K = 4

LANES = 128
SUBLANES = 8
VMEM_BYTES_V7X = 64 * 1024 * 1024
VMEM_LIMIT = VMEM_BYTES_V7X - 6 * 1024 * 1024

HEAD_DIM = 128
HGRN_CHUNK = 128
HGRN_SUB = 16
HGRN_STEP_ROWS = 256
S5_PACK = 8
S5_TT = 256

MOE_SUB = 256
MOE_ITEM_SUBS = 6
MOE_ITEM_ROWS = MOE_SUB * MOE_ITEM_SUBS


def _params(sem, vmem=VMEM_LIMIT):
    return pltpu.CompilerParams(dimension_semantics=sem, vmem_limit_bytes=vmem)


def _dot_nt(a, b):
    return lax.dot_general(a, b, (((1,), (1,)), ((), ())), preferred_element_type=F32)


def _dot_tn(a, b):
    return lax.dot_general(a, b, (((0,), (0,)), ((), ())), preferred_element_type=F32)


def _inproj_kernel(x_ref, g_ref, w_ref, z_ref, n_sc):
    @pl.when(pl.program_id(1) == 0)
    def _():
        x = x_ref[...]
        ms = jnp.mean(x * x, axis=-1, keepdims=True)
        n_sc[...] = (x * lax.rsqrt(ms + EPS) * g_ref[...]).astype(BF16)

    z_ref[...] = jnp.dot(n_sc[...], w_ref[...], preferred_element_type=F32)


def _in_proj(x, g, w_bf, tm=640, tn=512):
    n, d = x.shape
    cols = w_bf.shape[1]
    return pl.pallas_call(
        _inproj_kernel,
        out_shape=jax.ShapeDtypeStruct((n, cols), F32),
        grid=(n // tm, cols // tn),
        in_specs=[pl.BlockSpec((tm, d), lambda i, j: (i, 0)),
                  pl.BlockSpec((1, d), lambda i, j: (0, 0)),
                  pl.BlockSpec((d, tn), lambda i, j: (0, j))],
        out_specs=pl.BlockSpec((tm, tn), lambda i, j: (i, j)),
        scratch_shapes=[pltpu.VMEM((tm, d), BF16)],
        compiler_params=_params(("parallel", "arbitrary")),
        name="in_proj",
    )(x, g.reshape(1, d), w_bf)


def _gates(fz, lb):
    f = lb + (1.0 - lb) * jax.nn.sigmoid(fz)
    k = (1.0 - lb) * jax.nn.sigmoid(-fz)
    return f, k


def _onorm_gate(o, gon, gz):
    ms = jnp.mean(o * o, axis=-1, keepdims=True)
    return o * lax.rsqrt(ms + EPS) * gon * (gz * jax.nn.sigmoid(gz))


def _hgrn_prompt_kernel(q_ref, f_ref, v_ref, gz_ref, lb_ref, gon_ref, o_ref, s_ref, st_sc):
    c_idx = pl.program_id(2)
    C, SUB = HGRN_CHUNK, HGRN_SUB

    @pl.when(c_idx == 0)
    def _():
        st_sc[...] = jnp.zeros_like(st_sc)

    lb = lb_ref[...]
    gon = gon_ref[...]
    r = lax.broadcasted_iota(jnp.int32, (C, C), 0)
    c = lax.broadcasted_iota(jnp.int32, (C, C), 1)
    tri = (r >= c).astype(F32)
    rows = lax.broadcasted_iota(jnp.int32, (SUB, HEAD_DIM), 0)
    rfull = lax.broadcasted_iota(jnp.int32, (C, HEAD_DIM), 0)
    levels = []
    m = SUB
    while m < C:
        pair = jnp.logical_and((r // (2 * m)) == (c // (2 * m)),
                               jnp.logical_and((r % (2 * m)) >= m, (c % (2 * m)) < m))
        levels.append((m, pair, (rfull % (2 * m)) >= m))
        m *= 2

    for ci in range(HGRN_STEP_ROWS // C):
        sl = slice(ci * C, (ci + 1) * C)
        q, fz, v, gz = q_ref[sl, :], f_ref[sl, :], v_ref[sl, :], gz_ref[sl, :]
        f, k = _gates(fz, lb)
        b = jnp.dot(tri, jnp.log(f), precision=HIGHEST, preferred_element_type=F32)
        st = st_sc[...]
        o = _dot_nt(q * jnp.exp(b), st)
        att = jnp.zeros((C, C), F32)
        for m, pair, right in levels:
            bs = jnp.concatenate([jnp.broadcast_to(b[r0 + m - 1:r0 + m], (2 * m, HEAD_DIM))
                                  for r0 in range(0, C, 2 * m)], axis=0)
            e = jnp.exp(jnp.where(right, b - bs, bs - b))
            att = att + jnp.where(pair, _dot_nt(q * e, k * e), 0.0)
        o = o + jnp.dot(att, v, preferred_element_type=F32)
        parts = []
        for i in range(C // SUB):
            r0 = i * SUB
            qi, bi, ki, vi = q[r0:r0 + SUB], b[r0:r0 + SUB], k[r0:r0 + SUB], v[r0:r0 + SUB]
            oi = o[r0:r0 + SUB]
            for s in range(SUB):
                e = jnp.exp(jnp.where(rows >= s, bi - bi[s:s + 1], -jnp.inf))
                a_s = jnp.sum(qi * e * ki[s:s + 1], axis=-1, keepdims=True)
                oi = oi + a_s * vi[s:s + 1]
            parts.append(oi)
        o = jnp.concatenate(parts, axis=0)
        bl = b[C - 1:C]
        st = st * jnp.exp(bl) + _dot_tn(v, k * jnp.exp(bl - b))
        st_sc[...] = st
        o_ref[sl, :] = _onorm_gate(o, gon, gz).astype(o_ref.dtype)

    @pl.when(c_idx == pl.num_programs(2) - 1)
    def _():
        s_ref[...] = st_sc[...].T


def _hgrn_prompt(z, bsz, T, lb, g_onorm, heads):
    R = HGRN_STEP_ROWS
    col = lambda off: pl.BlockSpec((R, HEAD_DIM), lambda b, h, c: (b * (T // R) + c, off + h))
    vec = pl.BlockSpec((1, HEAD_DIM), lambda b, h, c: (0, h))
    return pl.pallas_call(
        _hgrn_prompt_kernel,
        out_shape=(jax.ShapeDtypeStruct((bsz, T, heads * HEAD_DIM), BF16),
                   jax.ShapeDtypeStruct((bsz, heads, HEAD_DIM, HEAD_DIM), F32)),
        grid=(bsz, heads, T // R),
        in_specs=[col(0), col(heads), col(2 * heads), col(3 * heads), vec, vec],
        out_specs=(pl.BlockSpec((None, R, HEAD_DIM), lambda b, h, c: (b, c, h)),
                   pl.BlockSpec((None, None, HEAD_DIM, HEAD_DIM), lambda b, h, c: (b, h, 0, 0))),
        scratch_shapes=[pltpu.VMEM((HEAD_DIM, HEAD_DIM), F32)],
        compiler_params=_params(("parallel", "parallel", "arbitrary")),
        name="hgrn_prompt",
    )(z, z, z, z, lb.reshape(1, -1), g_onorm.reshape(1, -1))


def _hgrn_sample_kernel(qT_ref, fT_ref, v_ref, gz_ref, lbT_ref, gon_ref, s0_ref, o_ref, s_ref, o_sc):
    lbT = lbT_ref[...]
    fT, kT = _gates(fT_ref[...], lbT)
    qT = qT_ref[...]
    v = v_ref[...]
    for bb in range(v.shape[0]):
        s_new = fT[:, bb:bb + 1] * s0_ref[bb] + kT[:, bb:bb + 1] * v[bb:bb + 1, :]
        s_ref[bb] = s_new
        o_sc[bb:bb + 1, :] = jnp.sum(qT[:, bb:bb + 1] * s_new, axis=0, keepdims=True)
    o_ref[...] = _onorm_gate(o_sc[...], gon_ref[...], gz_ref[...]).astype(o_ref.dtype)


def _hgrn_sample(z_s, s0, lb, g_onorm, heads):
    nb = bt = z_s.shape[0]
    z_sT = z_s[:, :2 * heads * HEAD_DIM].T
    lbT = jnp.broadcast_to(lb[:, None], (heads * HEAD_DIM, bt))
    return pl.pallas_call(
        _hgrn_sample_kernel,
        out_shape=(jax.ShapeDtypeStruct((nb, heads * HEAD_DIM), BF16),
                   jax.ShapeDtypeStruct(s0.shape, F32)),
        scratch_shapes=[pltpu.VMEM((bt, HEAD_DIM), F32)],
        grid=(heads, nb // bt),
        in_specs=[pl.BlockSpec((HEAD_DIM, bt), lambda h, b: (h, b)),
                  pl.BlockSpec((HEAD_DIM, bt), lambda h, b: (heads + h, b)),
                  pl.BlockSpec((bt, HEAD_DIM), lambda h, b: (b, 2 * heads + h)),
                  pl.BlockSpec((bt, HEAD_DIM), lambda h, b: (b, 3 * heads + h)),
                  pl.BlockSpec((HEAD_DIM, bt), lambda h, b: (h, 0)),
                  pl.BlockSpec((1, HEAD_DIM), lambda h, b: (0, h)),
                  pl.BlockSpec((bt, None, HEAD_DIM, HEAD_DIM), lambda h, b: (b, h, 0, 0))],
        out_specs=(pl.BlockSpec((bt, HEAD_DIM), lambda h, b: (b, h)),
                   pl.BlockSpec((bt, None, HEAD_DIM, HEAD_DIM), lambda h, b: (b, h, 0, 0))),
        compiler_params=_params(("parallel", "parallel")),
        name="hgrn_sample",
    )(z_sT, z_sT, z_s, z_s, lbT, g_onorm.reshape(1, -1), s0)


def _s5_tables(a_re, a_im, log_dt, b_re, b_im, c_re, c_im, d_skip, w1, b1, w2, b2, tt):
    G, P = a_re.shape
    HG = d_skip.shape[-1]
    SG = G // S5_PACK
    dt = jnp.exp(log_dt)
    ar, ai = a_re * dt, a_im * dt
    mag = jnp.exp(ar)
    abar_re, abar_im = mag * jnp.cos(ai), mag * jnp.sin(ai)
    den = a_re * a_re + a_im * a_im
    nr = abar_re - 1.0
    cr = (nr * a_re + abar_im * a_im) / den
    ci = (abar_im * a_re - nr * a_im) / den
    bb_re = cr[..., None] * b_re - ci[..., None] * b_im
    bb_im = cr[..., None] * b_im + ci[..., None] * b_re
    eye = jnp.eye(S5_PACK, dtype=F32)

    def bdiag_in(bb):
        t = jnp.einsum('sgph,gk->sghkp', bb.reshape(SG, S5_PACK, P, HG), eye)
        return t.reshape(SG, S5_PACK * HG, S5_PACK * P).astype(BF16)

    def bdiag_out(cc):
        t = jnp.einsum('sghp,gk->sgpkh', cc.reshape(SG, S5_PACK, HG, P), eye)
        return t.reshape(SG, S5_PACK * P, S5_PACK * HG).astype(BF16)

    def bdiag_sq(w):
        t = jnp.einsum('sghk,gj->sghjk', w.reshape(SG, S5_PACK, HG, HG), eye)
        return t.reshape(SG, S5_PACK * HG, S5_PACK * HG).astype(BF16)

    def powers(steps):
        st = steps.astype(F32)[:, None, None]
        pm = jnp.exp(st * ar)
        return ((pm * jnp.cos(st * ai)).reshape(len(steps), G * P),
                (pm * jnp.sin(st * ai)).reshape(len(steps), G * P))

    pw = powers(jnp.arange(1, tt // SUBLANES + 1))
    return dict(
        abar=(abar_re.reshape(1, G * P), abar_im.reshape(1, G * P)),
        bb=(bdiag_in(bb_re), bdiag_in(bb_im)), cc=(bdiag_out(c_re), bdiag_out(c_im)),
        d=d_skip.reshape(1, G * HG), w1=bdiag_sq(w1), b1=b1.reshape(1, G * HG),
        w2=bdiag_sq(w2), b2=b2.reshape(1, G * HG), pw=pw)


def _s5_readout(u, xr, xi, cre_ref, cim_ref, d_ref, w1_ref, b1_ref, w2_ref, b2_ref):
    y = (jnp.dot(xr.astype(BF16), cre_ref[...], preferred_element_type=F32)
         - jnp.dot(xi.astype(BF16), cim_ref[...], preferred_element_type=F32) + d_ref[...] * u)
    yg = jax.nn.gelu(y).astype(BF16)
    return ((jnp.dot(yg, w1_ref[...], preferred_element_type=F32) + b1_ref[...])
            * jax.nn.sigmoid(jnp.dot(yg, w2_ref[...], preferred_element_type=F32) + b2_ref[...]))


def _cmul_add(pr, pi, xr, xi, ar, ai):
    return pr * xr - pi * xi + ar, pr * xi + pi * xr + ai


def _s5_prompt_kernel(u_ref, bre_ref, bim_ref, cre_ref, cim_ref, d_ref, w1_ref, b1_ref, w2_ref, b2_ref,
                      pwre_ref, pwim_ref, o_ref, xr_ref, xi_ref, car_re, car_im, out_sc):
    t_idx = pl.program_id(2)
    tt = u_ref.shape[0]
    seg = tt // SUBLANES

    @pl.when(t_idx == 0)
    def _():
        car_re[...] = jnp.zeros_like(car_re)
        car_im[...] = jnp.zeros_like(car_im)

    u = jnp.concatenate([u_ref[pl.ds(j, SUBLANES, stride=seg), :] for j in range(seg)], axis=0)
    ub = u.astype(BF16)
    bur = jnp.dot(ub, bre_ref[...], preferred_element_type=F32)
    bui = jnp.dot(ub, bim_ref[...], preferred_element_type=F32)
    grp = lambda a, j: a[j * SUBLANES:(j + 1) * SUBLANES]
    a1r, a1i = pwre_ref[0:1, :], pwim_ref[0:1, :]
    lr, li = [grp(bur, 0)], [grp(bui, 0)]
    for j in range(1, seg):
        r, i = _cmul_add(a1r, a1i, lr[-1], li[-1], grp(bur, j), grp(bui, j))
        lr.append(r)
        li.append(i)
    asr, asi = pwre_ref[seg - 1:seg, :], pwim_ref[seg - 1:seg, :]
    sr, si = car_re[...], car_im[...]
    in_r, in_i = [sr], [si]
    for s in range(1, SUBLANES + 1):
        sr, si = _cmul_add(asr, asi, sr, si, lr[-1][s - 1:s], li[-1][s - 1:s])
        if s < SUBLANES:
            in_r.append(sr)
            in_i.append(si)
    car_re[...] = sr
    car_im[...] = si
    inr, ini = jnp.concatenate(in_r, axis=0), jnp.concatenate(in_i, axis=0)
    xs = [_cmul_add(pwre_ref[j:j + 1, :], pwim_ref[j:j + 1, :], inr, ini, lr[j], li[j]) for j in range(seg)]
    xr = jnp.concatenate([x[0] for x in xs], axis=0)
    xi = jnp.concatenate([x[1] for x in xs], axis=0)
    out = _s5_readout(u, xr, xi, cre_ref, cim_ref, d_ref, w1_ref, b1_ref, w2_ref, b2_ref)
    for j in range(seg):
        out_sc[pl.ds(j, SUBLANES, stride=seg), :] = grp(out, j)
    o_ref[...] = out_sc[...].astype(o_ref.dtype)

    @pl.when(t_idx == pl.num_programs(2) - 1)
    def _():
        xr_ref[...] = sr
        xi_ref[...] = si


def _s5_prompt(z, bsz, T, tb, u_col0):
    sg_n, uw, sw = tb['bb'][0].shape
    seg = tb['pw'][0].shape[0]
    tt = seg * SUBLANES
    per_sg = lambda shp: pl.BlockSpec((None,) + shp, lambda g, b, t: (g, 0, 0))
    vec = pl.BlockSpec((1, uw), lambda g, b, t: (0, g))
    pw = pl.BlockSpec((seg, sw), lambda g, b, t: (0, g))
    o_b, xr, xi = pl.pallas_call(
        _s5_prompt_kernel,
        out_shape=(jax.ShapeDtypeStruct((bsz, T, sg_n * uw), BF16),
                   jax.ShapeDtypeStruct((bsz, 1, sg_n * sw), F32),
                   jax.ShapeDtypeStruct((bsz, 1, sg_n * sw), F32)),
        grid=(sg_n, bsz, T // tt),
        in_specs=[pl.BlockSpec((tt, uw), lambda g, b, t: (b * (T // tt) + t, u_col0 + g)),
                  per_sg((uw, sw)), per_sg((uw, sw)), per_sg((sw, uw)), per_sg((sw, uw)),
                  vec, per_sg((uw, uw)), vec, per_sg((uw, uw)), vec, pw, pw],
        out_specs=(pl.BlockSpec((None, tt, uw), lambda g, b, t: (b, t, g)),
                   pl.BlockSpec((None, 1, sw), lambda g, b, t: (b, 0, g)),
                   pl.BlockSpec((None, 1, sw), lambda g, b, t: (b, 0, g))),
        scratch_shapes=[pltpu.VMEM((1, sw), F32), pltpu.VMEM((1, sw), F32), pltpu.VMEM((tt, uw), F32)],
        compiler_params=_params(("parallel", "parallel", "arbitrary")),
        name="s5_prompt",
    )(z, tb['bb'][0], tb['bb'][1], tb['cc'][0], tb['cc'][1], tb['d'], tb['w1'], tb['b1'], tb['w2'],
      tb['b2'], tb['pw'][0], tb['pw'][1])
    return o_b, xr, xi


def _s5_sample_kernel(u_ref, x0r_ref, x0i_ref, are_ref, aim_ref, bre_ref, bim_ref, cre_ref, cim_ref, d_ref,
                      w1_ref, b1_ref, w2_ref, b2_ref, o_ref, xr_ref, xi_ref):
    u = u_ref[...]
    ub = u.astype(BF16)
    ar, ai = are_ref[...], aim_ref[...]
    x0r, x0i = x0r_ref[...], x0i_ref[...]
    xr = jnp.dot(ub, bre_ref[...], preferred_element_type=F32) + ar * x0r - ai * x0i
    xi = jnp.dot(ub, bim_ref[...], preferred_element_type=F32) + ar * x0i + ai * x0r
    xr_ref[...] = xr
    xi_ref[...] = xi
    o_ref[...] = _s5_readout(u, xr, xi, cre_ref, cim_ref, d_ref, w1_ref, b1_ref, w2_ref, b2_ref).astype(o_ref.dtype)


def _s5_sample(z_s, x0r, x0i, tb, u_col0):
    nb = z_s.shape[0]
    sg_n, uw, sw = tb['bb'][0].shape
    per_sg = lambda shp: pl.BlockSpec((None,) + shp, lambda g: (g, 0, 0))
    vec = pl.BlockSpec((1, uw), lambda g: (0, g))
    st = pl.BlockSpec((nb, sw), lambda g: (0, g))
    svec = pl.BlockSpec((1, sw), lambda g: (0, g))
    return pl.pallas_call(
        _s5_sample_kernel,
        out_shape=(jax.ShapeDtypeStruct((nb, sg_n * uw), BF16),
                   jax.ShapeDtypeStruct((nb, sg_n * sw), F32),
                   jax.ShapeDtypeStruct((nb, sg_n * sw), F32)),
        grid=(sg_n,),
        in_specs=[pl.BlockSpec((nb, uw), lambda g: (0, u_col0 + g)), st, st, svec, svec,
                  per_sg((uw, sw)), per_sg((uw, sw)), per_sg((sw, uw)), per_sg((sw, uw)),
                  vec, per_sg((uw, uw)), vec, per_sg((uw, uw)), vec],
        out_specs=(pl.BlockSpec((nb, uw), lambda g: (0, g)), st, st),
        compiler_params=_params(("parallel",)),
        name="s5_sample",
    )(z_s, x0r, x0i, tb['abar'][0], tb['abar'][1], tb['bb'][0], tb['bb'][1], tb['cc'][0], tb['cc'][1],
      tb['d'], tb['w1'], tb['b1'], tb['w2'], tb['b2'])


def _pack_bf16_pair(hi, lo):
    bits = lambda a: lax.bitcast_convert_type(a.astype(BF16).astype(F32), jnp.uint32)
    return bits(hi) | (bits(lo) >> 16)


def _unpack_bf16_pair(packed):
    hi = lax.bitcast_convert_type(packed & jnp.uint32(0xFFFF0000), F32).astype(BF16)
    lo = lax.bitcast_convert_type(packed << 16, F32).astype(BF16)
    return hi, lo


def _outproj_kernel(oa_ref, ob_ref, x_ref, w_ref, g_ref, wr_ref, br_ref,
                    h1_ref, tok_ref, ti_ref, tw_ref, h1_sc, *, n_experts, row_chunk):
    j = pl.program_id(1)
    nj = pl.num_programs(1)
    tm, d = h1_sc.shape
    tn = w_ref.shape[1]
    half = oa_ref.shape[1]

    h = (x_ref[...] + jnp.dot(oa_ref[...], w_ref[:half, :], preferred_element_type=F32)
         + jnp.dot(ob_ref[...], w_ref[half:, :], preferred_element_type=F32))
    h1_ref[...] = h
    for jj in range(d // tn):
        @pl.when(j == jj)
        def _():
            h1_sc[:, jj * tn:(jj + 1) * tn] = h

    @pl.when(j == nj - 1)
    def _():
        for r0 in range(0, tm, row_chunk):
            rs = slice(r0, r0 + row_chunk)
            h1 = h1_sc[rs, :]
            ms = jnp.mean(h1 * h1, axis=-1, keepdims=True)
            tok = h1 * lax.rsqrt(ms + EPS) * g_ref[...]
            tok_ref[rs, :] = _pack_bf16_pair(tok[:, :d // 2], tok[:, d // 2:])
            logits = jnp.dot(tok, wr_ref[...], precision=HIGHEST, preferred_element_type=F32) + br_ref[...]
            lane = lax.broadcasted_iota(jnp.int32, logits.shape, 1).astype(F32)
            l = jnp.where(lane < n_experts, logits, -jnp.inf)
            vals, idxs = [], []
            for _ in range(TOP_K):
                m = jnp.max(l, axis=-1, keepdims=True)
                i = jnp.min(jnp.where(l == m, lane, float(LANES)), axis=-1, keepdims=True)
                vals.append(m)
                idxs.append(i)
                l = jnp.where(lane == i, -jnp.inf, l)
            es = [jnp.exp(vv - vals[0]) for vv in vals]
            tot = es[0] + es[1] + es[2] + es[3]
            ti = jnp.zeros(logits.shape, jnp.int32)
            tw = jnp.zeros(logits.shape, F32)
            for r in range(TOP_K):
                ti = jnp.where(lane == r, idxs[r].astype(jnp.int32), ti)
                tw = jnp.where(lane == r, es[r] / tot, tw)
            ti_ref[rs, :] = ti
            tw_ref[rs, :] = tw


def _out_proj(o_a, o_b, x, w_bf, g_ffn, w_router, b_router, tm=640, tn=512, row_chunk=128):
    n, d = x.shape
    half = o_a.shape[1]
    n_experts = w_router.shape[1]
    wr = jnp.zeros((d, LANES), F32).at[:, :n_experts].set(w_router)
    br = jnp.zeros((1, LANES), F32).at[0, :n_experts].set(b_router)
    row = lambda w: pl.BlockSpec((tm, w), lambda i, j: (i, 0))
    return pl.pallas_call(
        functools.partial(_outproj_kernel, n_experts=n_experts, row_chunk=row_chunk),
        out_shape=(jax.ShapeDtypeStruct((n, d), F32), jax.ShapeDtypeStruct((n, d // 2), jnp.uint32),
                   jax.ShapeDtypeStruct((n, LANES), jnp.int32), jax.ShapeDtypeStruct((n, LANES), F32)),
        grid=(n // tm, d // tn),
        in_specs=[row(half), row(half),
                  pl.BlockSpec((tm, tn), lambda i, j: (i, j)),
                  pl.BlockSpec((d, tn), lambda i, j: (0, j)),
                  pl.BlockSpec((1, d), lambda i, j: (0, 0)),
                  pl.BlockSpec((d, LANES), lambda i, j: (0, 0)),
                  pl.BlockSpec((1, LANES), lambda i, j: (0, 0))],
        out_specs=(pl.BlockSpec((tm, tn), lambda i, j: (i, j)), row(d // 2), row(LANES), row(LANES)),
        scratch_shapes=[pltpu.VMEM((tm, d), F32)],
        compiler_params=_params(("parallel", "arbitrary")),
        name="out_proj_router",
    )(o_a, o_b, x, w_bf, g_ffn.reshape(1, d), wr, br)


def _moe_plan(top_i, n_experts):
    n_assign = top_i.size
    flat_e = top_i.reshape(-1)
    onehot = (flat_e[:, None] == jnp.arange(n_experts, dtype=jnp.int32)[None, :]).astype(jnp.int32)
    counts = onehot.sum(axis=0)
    rank = jnp.take_along_axis(jnp.cumsum(onehot, axis=0) - onehot, flat_e[:, None], axis=1)[:, 0]
    nsub_e = (counts + MOE_SUB - 1) // MOE_SUB
    group_start = (jnp.cumsum(nsub_e) - nsub_e) * MOE_SUB
    nitems_e = (nsub_e + MOE_ITEM_SUBS - 1) // MOE_ITEM_SUBS
    item_end = jnp.cumsum(nitems_e)
    item_start = item_end - nitems_e
    n_items = n_assign // MOE_ITEM_ROWS + 1 + n_experts
    m_pad = n_assign + n_experts * MOE_SUB
    w = jnp.arange(n_items, dtype=jnp.int32)
    total = item_end[-1]
    w_eff = jnp.minimum(w, total - 1)
    item_e = jnp.minimum(jnp.sum((item_end[None, :] <= w_eff[:, None]).astype(jnp.int32), axis=1), n_experts - 1)
    local = w_eff - item_start[item_e]
    item_nsub = jnp.where(w < total, jnp.clip(nsub_e[item_e] - local * MOE_ITEM_SUBS, 0, MOE_ITEM_SUBS), 0)
    item_row0 = group_start[item_e] + local * MOE_ITEM_ROWS
    item_slot = w_eff
    cpos = group_start[flat_e] + rank
    ipos = (item_start[flat_e] + rank // MOE_ITEM_ROWS) * MOE_ITEM_ROWS + rank % MOE_ITEM_ROWS
    src_token = jnp.zeros((m_pad,), jnp.int32).at[cpos].set(
        jnp.arange(n_assign, dtype=jnp.int32) // top_i.shape[1])
    meta = jnp.stack([item_e, item_row0, item_nsub, item_slot]).astype(jnp.int32)
    ipos_kmajor = ipos.reshape(-1, top_i.shape[1]).T
    return meta, src_token, ipos_kmajor, n_items


def _moe_up_kernel(meta_ref, xs_hbm, wg_ref, wu_ref, bg_ref, bu_ref, act_ref,
                   stage_sc, lhs_sc, wg_sc, wu_sc, sems):
    w, j = pl.program_id(0), pl.program_id(1)
    row0 = meta_ref[1, w]
    nsub = meta_ref[2, w]
    half = stage_sc.shape[1]

    def lhs_copy(s):
        src = xs_hbm.at[pl.ds(pl.multiple_of(row0 + s * MOE_SUB, MOE_SUB), MOE_SUB), :]
        return pltpu.make_async_copy(src, stage_sc.at[pl.ds(s * MOE_SUB, MOE_SUB), :], sems.at[s])

    @pl.when(j == 0)
    def _():
        for s in range(MOE_ITEM_SUBS):
            @pl.when(s < nsub)
            def _():
                lhs_copy(s).start()
        for s in range(MOE_ITEM_SUBS):
            @pl.when(s < nsub)
            def _():
                lhs_copy(s).wait()
                rows = slice(s * MOE_SUB, (s + 1) * MOE_SUB)
                hi, lo = _unpack_bf16_pair(stage_sc[rows, :])
                lhs_sc[rows, :half] = hi
                lhs_sc[rows, half:] = lo

    @pl.when(nsub > 0)
    def _():
        wg_sc[...] = wg_ref[...].astype(BF16)
        wu_sc[...] = wu_ref[...].astype(BF16)

    for s in range(MOE_ITEM_SUBS):
        rows = slice(s * MOE_SUB, (s + 1) * MOE_SUB)

        @pl.when(s < nsub)
        def _():
            x = lhs_sc[rows, :]
            gate = jnp.dot(x, wg_sc[...], preferred_element_type=F32) + bg_ref[...]
            up = jnp.dot(x, wu_sc[...], preferred_element_type=F32) + bu_ref[...]
            gate = jnp.minimum(gate, SWIGLU_LIMIT)
            up = jnp.clip(up, -SWIGLU_LIMIT, SWIGLU_LIMIT)
            act_ref[rows, :] = ((up + 1.0) * gate * jax.nn.sigmoid(SWIGLU_ALPHA * gate)).astype(BF16)

        @pl.when(jnp.logical_and(s >= nsub, nsub > 0))
        def _():
            act_ref[rows, :] = jnp.zeros((MOE_SUB, act_ref.shape[1]), BF16)


def _moe_up(meta, xs, w_gate_up, b_gate_up, n_items, tn=256):
    n_experts, d, two_ff = w_gate_up.shape
    d_ff = two_ff // 2
    nj = d_ff // tn
    jj = lambda w, j, m: jnp.where(m[2, w] > 0, j, nj - 1)
    grid_spec = pltpu.PrefetchScalarGridSpec(
        num_scalar_prefetch=1,
        grid=(n_items, nj),
        in_specs=[pl.BlockSpec(memory_space=pl.ANY),
                  pl.BlockSpec((None, d, tn), lambda w, j, m: (m[0, w], 0, jj(w, j, m))),
                  pl.BlockSpec((None, d, tn), lambda w, j, m: (m[0, w], 0, nj + jj(w, j, m))),
                  pl.BlockSpec((None, 1, tn), lambda w, j, m: (m[0, w], 0, jj(w, j, m))),
                  pl.BlockSpec((None, 1, tn), lambda w, j, m: (m[0, w], 0, nj + jj(w, j, m)))],
        out_specs=pl.BlockSpec((None, MOE_ITEM_ROWS, tn), lambda w, j, m: (m[3, w], 0, jj(w, j, m))),
        scratch_shapes=[pltpu.VMEM((MOE_ITEM_ROWS, d // 2), jnp.uint32), pltpu.VMEM((MOE_ITEM_ROWS, d), BF16),
                        pltpu.VMEM((d, tn), BF16), pltpu.VMEM((d, tn), BF16),
                        pltpu.SemaphoreType.DMA((MOE_ITEM_SUBS,))])
    bgu = b_gate_up.reshape(n_experts, 1, two_ff)
    return pl.pallas_call(
        _moe_up_kernel,
        out_shape=jax.ShapeDtypeStruct((n_items, MOE_ITEM_ROWS, d_ff), BF16),
        grid_spec=grid_spec,
        compiler_params=_params(("arbitrary", "arbitrary")),
        name="moe_gate_up",
    )(meta, xs, w_gate_up, w_gate_up, bgu, bgu)


def _moe_down_kernel(meta_ref, act_ref, wd_ref, bd_ref, y_ref, wd_sc):
    w = pl.program_id(0)
    nsub = meta_ref[2, w]

    @pl.when(nsub > 0)
    def _():
        wd_sc[...] = wd_ref[...].astype(BF16)

    for s in range(MOE_ITEM_SUBS):
        rows = slice(s * MOE_SUB, (s + 1) * MOE_SUB)

        @pl.when(s < nsub)
        def _():
            y_ref[rows, :] = jnp.dot(act_ref[rows, :], wd_sc[...], preferred_element_type=F32) + bd_ref[...]

        @pl.when(jnp.logical_and(s >= nsub, nsub > 0))
        def _():
            y_ref[rows, :] = jnp.zeros((MOE_SUB, y_ref.shape[1]), F32)


def _moe_down(meta, act, w_down, b_down, tn=256):
    n_experts, d_ff, d = w_down.shape
    n_items = act.shape[0]
    nj = d // tn
    jj = lambda w, j, m: jnp.where(m[2, w] > 0, j, nj - 1)
    grid_spec = pltpu.PrefetchScalarGridSpec(
        num_scalar_prefetch=1,
        grid=(n_items, nj),
        in_specs=[pl.BlockSpec((None, MOE_ITEM_ROWS, d_ff), lambda w, j, m: (m[3, w], 0, 0)),
                  pl.BlockSpec((None, d_ff, tn), lambda w, j, m: (m[0, w], 0, jj(w, j, m))),
                  pl.BlockSpec((None, 1, tn), lambda w, j, m: (m[0, w], 0, jj(w, j, m)))],
        out_specs=pl.BlockSpec((None, MOE_ITEM_ROWS, tn), lambda w, j, m: (m[3, w], 0, jj(w, j, m))),
        scratch_shapes=[pltpu.VMEM((d_ff, tn), BF16)])
    return pl.pallas_call(
        _moe_down_kernel,
        out_shape=jax.ShapeDtypeStruct((n_items, MOE_ITEM_ROWS, d), F32),
        grid_spec=grid_spec,
        compiler_params=_params(("arbitrary", "arbitrary")),
        name="moe_down",
    )(meta, act, w_down, b_down.reshape(n_experts, 1, d))


def _combine_norm_kernel(h1_ref, y0_ref, y1_ref, y2_ref, y3_ref, tw_ref, g_ref, h2_ref, n_ref):
    tw = tw_ref[...]
    h2 = h1_ref[...]
    for k, y_ref in enumerate((y0_ref, y1_ref, y2_ref, y3_ref)):
        h2 = h2 + tw[:, k:k + 1] * y_ref[...]
    ms = jnp.mean(h2 * h2, axis=-1, keepdims=True)
    h2_ref[...] = h2
    n_ref[...] = (h2 * lax.rsqrt(ms + EPS) * g_ref[...]).astype(BF16)


def _combine_norm(h1, y_sel, tw, g_ple, tm=160):
    n, d = h1.shape
    row = pl.BlockSpec((tm, d), lambda i: (i, 0))
    sel = lambda k: pl.BlockSpec((None, tm, d), lambda i: (k, i, 0))
    return pl.pallas_call(
        _combine_norm_kernel,
        out_shape=(jax.ShapeDtypeStruct((n, d), F32), jax.ShapeDtypeStruct((n, d), BF16)),
        grid=(n // tm,),
        in_specs=[row, sel(0), sel(1), sel(2), sel(3), pl.BlockSpec((tm, LANES), lambda i: (i, 0)),
                  pl.BlockSpec((1, d), lambda i: (0, 0))],
        out_specs=(row, row),
        compiler_params=_params(("parallel",)),
        name="moe_combine_norm",
    )(h1, y_sel, y_sel, y_sel, y_sel, tw, g_ple.reshape(1, d))


def _ple_kernel(n_ref, p_ref, h2_ref, wg_ref, wp_ref, gf_ref, y_ref):
    j = pl.program_id(1)
    tn = wg_ref.shape[1]
    gate = jax.nn.sigmoid(jnp.dot(n_ref[...], wg_ref[...], preferred_element_type=F32))
    emb = jnp.dot(p_ref[...].astype(BF16), wp_ref[...], preferred_element_type=F32)
    h3 = h2_ref[...] + gate * emb
    for jj in range(y_ref.shape[1] // tn):
        @pl.when(j == jj)
        def _():
            y_ref[:, jj * tn:(jj + 1) * tn] = h3

    @pl.when(j == pl.num_programs(1) - 1)
    def _():
        for r0 in range(0, y_ref.shape[0], 128):
            h = y_ref[r0:r0 + 128, :]
            ms = jnp.mean(h * h, axis=-1, keepdims=True)
            y_ref[r0:r0 + 128, :] = h * lax.rsqrt(ms + EPS) * gf_ref[...]


def _ple(n3, p, h2, wg_bf, wp_bf, g_final, tm=640, tn=512):
    n, d = h2.shape
    pd = p.shape[1]
    return pl.pallas_call(
        _ple_kernel,
        out_shape=jax.ShapeDtypeStruct((n, d), F32),
        grid=(n // tm, d // tn),
        in_specs=[pl.BlockSpec((tm, d), lambda i, j: (i, 0)),
                  pl.BlockSpec((tm, pd), lambda i, j: (i, 0)),
                  pl.BlockSpec((tm, tn), lambda i, j: (i, j)),
                  pl.BlockSpec((d, tn), lambda i, j: (0, j)),
                  pl.BlockSpec((pd, tn), lambda i, j: (0, j)),
                  pl.BlockSpec((1, d), lambda i, j: (0, 0))],
        out_specs=pl.BlockSpec((tm, d), lambda i, j: (i, 0)),
        compiler_params=_params(("parallel", "arbitrary")),
        name="ple_final",
    )(n3, p, h2, wg_bf, wp_bf, g_final.reshape(1, d))


def kernel(x_prompt, x_sample, state_hgrn, state_s5_re, state_s5_im, p_prompt, p_sample, g_mix, w_in, hgrn_gamma, g_onorm, s5_a_re, s5_a_im, s5_log_dt, s5_b_re, s5_b_im, s5_c_re, s5_c_im, s5_d, glu_w1, glu_b1, glu_w2, glu_b2, w_out, g_ffn, w_router, b_router, w_gate_up, b_gate_up, w_down, b_down, g_ple, w_ple_gate, w_ple_proj, g_final):
    assert w_in.shape[0] == 1, "single-layer trunk"
    bsz, T, d = x_prompt.shape
    nb = x_sample.shape[0]
    n_prompt = bsz * T
    heads = hgrn_gamma.shape[1] // HEAD_DIM
    G, P = s5_a_re.shape[1:]
    n_experts = w_router.shape[-1]
    u_col0 = 4 * heads

    lb = jnp.cumsum(jax.nn.softmax(hgrn_gamma.astype(F32), axis=0), axis=0)[0]
    h = jnp.concatenate([x_prompt.reshape(n_prompt, d), x_sample.reshape(nb, d)], axis=0)
    z = _in_proj(h, g_mix[0], w_in[0].astype(BF16))
    z_s = z[n_prompt:]
    oa_p, s_p = _hgrn_prompt(z, bsz, T, lb, g_onorm[0], heads)
    oa_s, s_s = _hgrn_sample(z_s, state_hgrn[0], lb, g_onorm[0], heads)
    tb = _s5_tables(s5_a_re[0], s5_a_im[0], s5_log_dt[0], s5_b_re[0], s5_b_im[0], s5_c_re[0], s5_c_im[0],
                    s5_d[0], glu_w1[0], glu_b1[0], glu_w2[0], glu_b2[0], S5_TT)
    ob_p, xr_p, xi_p = _s5_prompt(z, bsz, T, tb, u_col0)
    ob_s, xr_s, xi_s = _s5_sample(z_s, state_s5_re[0].reshape(nb, G * P), state_s5_im[0].reshape(nb, G * P),
                                  tb, u_col0)
    o_a = jnp.concatenate([oa_p.reshape(n_prompt, -1), oa_s], axis=0)
    o_b = jnp.concatenate([ob_p.reshape(n_prompt, -1), ob_s], axis=0)
    h1, tok, ti, tw = _out_proj(o_a, o_b, h, w_out[0].astype(BF16), g_ffn[0], w_router[0], b_router[0])
    meta, src_token, ipos, n_items = _moe_plan(ti[:, :TOP_K], n_experts)
    xs = tok.at[src_token].get(mode='promise_in_bounds')
    act = _moe_up(meta, xs, w_gate_up[0], b_gate_up[0], n_items)
    y_items = _moe_down(meta, act, w_down[0], b_down[0])
    y_sel = y_items.reshape(-1, d).at[ipos.reshape(-1)].get(mode='promise_in_bounds')
    h2, n3 = _combine_norm(h1, y_sel.reshape(TOP_K, -1, d), tw, g_ple[0])
    p_all = jnp.concatenate([p_prompt[0].reshape(n_prompt, -1), p_sample[0].reshape(nb, -1)], axis=0)
    y = _ple(n3, p_all, h2, w_ple_gate[0].astype(BF16), w_ple_proj[0].astype(BF16), g_final)
    return (y[:n_prompt].reshape(bsz, T, d), y[n_prompt:].reshape(nb, 1, d),
            s_p[None], xr_p.reshape(1, bsz, G, P), xi_p.reshape(1, bsz, G, P),
            s_s[None], xr_s.reshape(1, nb, G, P), xi_s.reshape(1, nb, G, P))
```

```python
import functools
import math

import jax
import jax.numpy as jnp
from jax import lax
from jax.experimental import pallas as pl
from jax.experimental.pallas import tpu as pltpu

F32 = jnp.float32
BF16 = jnp.bfloat16
HIGHEST = lax.Precision.HIGHEST

EPS = 1e-6
SWIGLU_LIMIT = 7.0
SWIGLU_ALPHA = 1.702
TOP_K = 4

LANES = 128
SUBLANES = 8
VMEM_BYTES_V7X = 64 * 1024 * 1024
VMEM_LIMIT = VMEM_BYTES_V7X - 6 * 1024 * 1024

HEAD_DIM = 128
HGRN_CHUNK = 128
HGRN_SUB = 8
HGRN_STEP_ROWS = 256
S5_PACK = 8
S5_TT = 256

MOE_SUB = 256
MOE_ITEM_SUBS = 6
MOE_ITEM_ROWS = MOE_SUB * MOE_ITEM_SUBS


def _params(sem, vmem=VMEM_LIMIT):
    return pltpu.CompilerParams(dimension_semantics=sem, vmem_limit_bytes=vmem)


def _dot_nt(a, b):
    return lax.dot_general(a, b, (((1,), (1,)), ((), ())), preferred_element_type=F32)


def _dot_tn(a, b):
    return lax.dot_general(a, b, (((0,), (0,)), ((), ())), preferred_element_type=F32)


def _inproj_kernel(x_ref, g_ref, w_ref, z_ref, n_sc):
    @pl.when(pl.program_id(1) == 0)
    def _():
        for r0 in range(0, x_ref.shape[0], 128):
            x = x_ref[r0:r0 + 128, :]
            ms = jnp.mean(x * x, axis=-1, keepdims=True)
            n_sc[r0:r0 + 128, :] = (x * lax.rsqrt(ms + EPS) * g_ref[...]).astype(BF16)

    z_ref[...] = jnp.dot(n_sc[...], w_ref[...], preferred_element_type=F32)


def _in_proj(x, g, w_bf, tm=640, tn=1024):
    n, d = x.shape
    cols = w_bf.shape[1]
    return pl.pallas_call(
        _inproj_kernel,
        out_shape=jax.ShapeDtypeStruct((n, cols), F32),
        grid=(n // tm, cols // tn),
        in_specs=[pl.BlockSpec((tm, d), lambda i, j: (i, 0)),
                  pl.BlockSpec((1, d), lambda i, j: (0, 0)),
                  pl.BlockSpec((d, tn), lambda i, j: (0, j))],
        out_specs=pl.BlockSpec((tm, tn), lambda i, j: (i, j)),
        scratch_shapes=[pltpu.VMEM((tm, d), BF16)],
        compiler_params=_params(("parallel", "arbitrary")),
        name="in_proj",
    )(x, g.reshape(1, d), w_bf)


def _gates(fz, lb):
    f = lb + (1.0 - lb) * jax.nn.sigmoid(fz)
    k = (1.0 - lb) * jax.nn.sigmoid(-fz)
    return f, k


def _onorm_gate(o, gon, gz):
    ms = jnp.mean(o * o, axis=-1, keepdims=True)
    return o * lax.rsqrt(ms + EPS) * gon * (gz * jax.nn.sigmoid(gz))


def _hgrn_prompt_kernel(q_ref, f_ref, v_ref, gz_ref, lb_ref, gon_ref, o_ref, s_ref, st_sc):
    c_idx = pl.program_id(2)
    C, SUB = HGRN_CHUNK, HGRN_SUB

    @pl.when(c_idx == 0)
    def _():
        st_sc[...] = jnp.zeros_like(st_sc)

    lb = lb_ref[...]
    gon = gon_ref[...]
    r = lax.broadcasted_iota(jnp.int32, (C, C), 0)
    c = lax.broadcasted_iota(jnp.int32, (C, C), 1)
    tri = (r >= c).astype(F32)
    rows = lax.broadcasted_iota(jnp.int32, (SUB, HEAD_DIM), 0)
    rfull = lax.broadcasted_iota(jnp.int32, (C, HEAD_DIM), 0)
    levels = []
    m = SUB
    while m < C:
        pair = jnp.logical_and((r // (2 * m)) == (c // (2 * m)),
                               jnp.logical_and((r % (2 * m)) >= m, (c % (2 * m)) < m))
        levels.append((m, pair, (rfull % (2 * m)) >= m))
        m *= 2

    for ci in range(HGRN_STEP_ROWS // C):
        sl = slice(ci * C, (ci + 1) * C)
        q, fz, v, gz = q_ref[sl, :], f_ref[sl, :], v_ref[sl, :], gz_ref[sl, :]
        f, k = _gates(fz, lb)
        b = jnp.dot(tri, jnp.log(f), precision=HIGHEST, preferred_element_type=F32)
        st = st_sc[...]
        o = _dot_nt(q * jnp.exp(b), st)
        att = jnp.zeros((C, C), F32)
        for m, pair, right in levels:
            bs = jnp.concatenate([jnp.broadcast_to(b[r0 + m - 1:r0 + m], (2 * m, HEAD_DIM))
                                  for r0 in range(0, C, 2 * m)], axis=0)
            e = jnp.exp(jnp.where(right, b - bs, bs - b))
            att = att + jnp.where(pair, _dot_nt(q * e, k * e), 0.0)
        o = o + jnp.dot(att, v, preferred_element_type=F32)
        parts = []
        for i in range(C // SUB):
            r0 = i * SUB
            qi, bi, ki, vi = q[r0:r0 + SUB], b[r0:r0 + SUB], k[r0:r0 + SUB], v[r0:r0 + SUB]
            oi = o[r0:r0 + SUB]
            for s in range(SUB):
                e = jnp.exp(jnp.where(rows >= s, bi - bi[s:s + 1], -jnp.inf))
                a_s = jnp.sum(qi * e * ki[s:s + 1], axis=-1, keepdims=True)
                oi = oi + a_s * vi[s:s + 1]
            parts.append(oi)
        o = jnp.concatenate(parts, axis=0)
        bl = b[C - 1:C]
        st = st * jnp.exp(bl) + _dot_tn(v, k * jnp.exp(bl - b))
        st_sc[...] = st
        o_ref[sl, :] = _onorm_gate(o, gon, gz).astype(o_ref.dtype)

    @pl.when(c_idx == pl.num_programs(2) - 1)
    def _():
        s_ref[...] = st_sc[...].T


def _hgrn_prompt(z, bsz, T, lb, g_onorm, heads):
    R = HGRN_STEP_ROWS
    col = lambda off: pl.BlockSpec((R, HEAD_DIM), lambda b, h, c: (b * (T // R) + c, off + h))
    vec = pl.BlockSpec((1, HEAD_DIM), lambda b, h, c: (0, h))
    return pl.pallas_call(
        _hgrn_prompt_kernel,
        out_shape=(jax.ShapeDtypeStruct((bsz, T, heads * HEAD_DIM), BF16),
                   jax.ShapeDtypeStruct((bsz, heads, HEAD_DIM, HEAD_DIM), F32)),
        grid=(bsz, heads, T // R),
        in_specs=[col(0), col(heads), col(2 * heads), col(3 * heads), vec, vec],
        out_specs=(pl.BlockSpec((None, R, HEAD_DIM), lambda b, h, c: (b, c, h)),
                   pl.BlockSpec((None, None, HEAD_DIM, HEAD_DIM), lambda b, h, c: (b, h, 0, 0))),
        scratch_shapes=[pltpu.VMEM((HEAD_DIM, HEAD_DIM), F32)],
        compiler_params=_params(("parallel", "parallel", "arbitrary")),
        name="hgrn_prompt",
    )(z, z, z, z, lb.reshape(1, -1), g_onorm.reshape(1, -1))


def _hgrn_sample_kernel(qT_ref, fT_ref, v_ref, gz_ref, lbT_ref, gon_ref, s0_ref, o_ref, s_ref, o_sc):
    lbT = lbT_ref[...]
    fT, kT = _gates(fT_ref[...], lbT)
    qT = qT_ref[...]
    v = v_ref[...]
    for bb in range(v.shape[0]):
        s_new = fT[:, bb:bb + 1] * s0_ref[bb] + kT[:, bb:bb + 1] * v[bb:bb + 1, :]
        s_ref[bb] = s_new
        o_sc[bb:bb + 1, :] = jnp.sum(qT[:, bb:bb + 1] * s_new, axis=0, keepdims=True)
    o_ref[...] = _onorm_gate(o_sc[...], gon_ref[...], gz_ref[...]).astype(o_ref.dtype)


def _hgrn_sample(z_s, s0, lb, g_onorm, heads):
    nb = bt = z_s.shape[0]
    z_sT = z_s[:, :2 * heads * HEAD_DIM].T
    lbT = jnp.broadcast_to(lb[:, None], (heads * HEAD_DIM, bt))
    return pl.pallas_call(
        _hgrn_sample_kernel,
        out_shape=(jax.ShapeDtypeStruct((nb, heads * HEAD_DIM), BF16),
                   jax.ShapeDtypeStruct(s0.shape, F32)),
        scratch_shapes=[pltpu.VMEM((bt, HEAD_DIM), F32)],
        grid=(heads, nb // bt),
        in_specs=[pl.BlockSpec((HEAD_DIM, bt), lambda h, b: (h, b)),
                  pl.BlockSpec((HEAD_DIM, bt), lambda h, b: (heads + h, b)),
                  pl.BlockSpec((bt, HEAD_DIM), lambda h, b: (b, 2 * heads + h)),
                  pl.BlockSpec((bt, HEAD_DIM), lambda h, b: (b, 3 * heads + h)),
                  pl.BlockSpec((HEAD_DIM, bt), lambda h, b: (h, 0)),
                  pl.BlockSpec((1, HEAD_DIM), lambda h, b: (0, h)),
                  pl.BlockSpec((bt, None, HEAD_DIM, HEAD_DIM), lambda h, b: (b, h, 0, 0))],
        out_specs=(pl.BlockSpec((bt, HEAD_DIM), lambda h, b: (b, h)),
                   pl.BlockSpec((bt, None, HEAD_DIM, HEAD_DIM), lambda h, b: (b, h, 0, 0))),
        compiler_params=_params(("parallel", "parallel")),
        name="hgrn_sample",
    )(z_sT, z_sT, z_s, z_s, lbT, g_onorm.reshape(1, -1), s0)


def _s5_tables(a_re, a_im, log_dt, b_re, b_im, c_re, c_im, d_skip, w1, b1, w2, b2, tt):
    G, P = a_re.shape
    HG = d_skip.shape[-1]
    SG = G // S5_PACK
    dt = jnp.exp(log_dt)
    ar, ai = a_re * dt, a_im * dt
    mag = jnp.exp(ar)
    abar_re, abar_im = mag * jnp.cos(ai), mag * jnp.sin(ai)
    den = a_re * a_re + a_im * a_im
    nr = abar_re - 1.0
    cr = (nr * a_re + abar_im * a_im) / den
    ci = (abar_im * a_re - nr * a_im) / den
    bb_re = cr[..., None] * b_re - ci[..., None] * b_im
    bb_im = cr[..., None] * b_im + ci[..., None] * b_re
    eye = jnp.eye(S5_PACK, dtype=F32)

    def bdiag_in(bb):
        t = jnp.einsum('sgph,gk->sghkp', bb.reshape(SG, S5_PACK, P, HG), eye)
        return t.reshape(SG, S5_PACK * HG, S5_PACK * P).astype(BF16)

    def bdiag_out(cc):
        t = jnp.einsum('sghp,gk->sgpkh', cc.reshape(SG, S5_PACK, HG, P), eye)
        return t.reshape(SG, S5_PACK * P, S5_PACK * HG).astype(BF16)

    def bdiag_sq(w):
        t = jnp.einsum('sghk,gj->sghjk', w.reshape(SG, S5_PACK, HG, HG), eye)
        return t.reshape(SG, S5_PACK * HG, S5_PACK * HG).astype(BF16)

    def powers(steps):
        st = steps.astype(F32)[:, None, None]
        pm = jnp.exp(st * ar)
        return ((pm * jnp.cos(st * ai)).reshape(len(steps), G * P),
                (pm * jnp.sin(st * ai)).reshape(len(steps), G * P))

    pw = powers(jnp.arange(1, tt // SUBLANES + 1))
    return dict(
        abar=(abar_re.reshape(1, G * P), abar_im.reshape(1, G * P)),
        bb=(bdiag_in(bb_re), bdiag_in(bb_im)), cc=(bdiag_out(c_re), bdiag_out(c_im)),
        d=d_skip.reshape(1, G * HG), w1=bdiag_sq(w1), b1=b1.reshape(1, G * HG),
        w2=bdiag_sq(w2), b2=b2.reshape(1, G * HG), pw=pw)


def _s5_readout(u, xr, xi, cre_ref, cim_ref, d_ref, w1_ref, b1_ref, w2_ref, b2_ref):
    y = (jnp.dot(xr.astype(BF16), cre_ref[...], preferred_element_type=F32)
         - jnp.dot(xi.astype(BF16), cim_ref[...], preferred_element_type=F32) + d_ref[...] * u)
    yg = jax.nn.gelu(y).astype(BF16)
    return ((jnp.dot(yg, w1_ref[...], preferred_element_type=F32) + b1_ref[...])
            * jax.nn.sigmoid(jnp.dot(yg, w2_ref[...], preferred_element_type=F32) + b2_ref[...]))


def _cmul_add(pr, pi, xr, xi, ar, ai):
    return pr * xr - pi * xi + ar, pr * xi + pi * xr + ai


def _s5_prompt_kernel(u_ref, bre_ref, bim_ref, cre_ref, cim_ref, d_ref, w1_ref, b1_ref, w2_ref, b2_ref,
                      pwre_ref, pwim_ref, o_ref, xr_ref, xi_ref, car_re, car_im, out_sc):
    t_idx = pl.program_id(2)
    tt = u_ref.shape[0]
    seg = tt // SUBLANES

    @pl.when(t_idx == 0)
    def _():
        car_re[...] = jnp.zeros_like(car_re)
        car_im[...] = jnp.zeros_like(car_im)

    u = jnp.concatenate([u_ref[pl.ds(j, SUBLANES, stride=seg), :] for j in range(seg)], axis=0)
    ub = u.astype(BF16)
    bur = jnp.dot(ub, bre_ref[...], preferred_element_type=F32)
    bui = jnp.dot(ub, bim_ref[...], preferred_element_type=F32)
    grp = lambda a, j: a[j * SUBLANES:(j + 1) * SUBLANES]
    a1r, a1i = pwre_ref[0:1, :], pwim_ref[0:1, :]
    lr, li = [grp(bur, 0)], [grp(bui, 0)]
    for j in range(1, seg):
        r, i = _cmul_add(a1r, a1i, lr[-1], li[-1], grp(bur, j), grp(bui, j))
        lr.append(r)
        li.append(i)
    asr, asi = pwre_ref[seg - 1:seg, :], pwim_ref[seg - 1:seg, :]
    sr, si = car_re[...], car_im[...]
    in_r, in_i = [sr], [si]
    for s in range(1, SUBLANES + 1):
        sr, si = _cmul_add(asr, asi, sr, si, lr[-1][s - 1:s], li[-1][s - 1:s])
        if s < SUBLANES:
            in_r.append(sr)
            in_i.append(si)
    car_re[...] = sr
    car_im[...] = si
    inr, ini = jnp.concatenate(in_r, axis=0), jnp.concatenate(in_i, axis=0)
    xs = [_cmul_add(pwre_ref[j:j + 1, :], pwim_ref[j:j + 1, :], inr, ini, lr[j], li[j]) for j in range(seg)]
    xr = jnp.concatenate([x[0] for x in xs], axis=0)
    xi = jnp.concatenate([x[1] for x in xs], axis=0)
    out = _s5_readout(u, xr, xi, cre_ref, cim_ref, d_ref, w1_ref, b1_ref, w2_ref, b2_ref)
    for j in range(seg):
        out_sc[pl.ds(j, SUBLANES, stride=seg), :] = grp(out, j)
    o_ref[...] = out_sc[...].astype(o_ref.dtype)

    @pl.when(t_idx == pl.num_programs(2) - 1)
    def _():
        xr_ref[...] = sr
        xi_ref[...] = si


def _s5_prompt(z, bsz, T, tb, u_col0):
    sg_n, uw, sw = tb['bb'][0].shape
    seg = tb['pw'][0].shape[0]
    tt = seg * SUBLANES
    per_sg = lambda shp: pl.BlockSpec((None,) + shp, lambda g, b, t: (g, 0, 0))
    vec = pl.BlockSpec((1, uw), lambda g, b, t: (0, g))
    pw = pl.BlockSpec((seg, sw), lambda g, b, t: (0, g))
    o_b, xr, xi = pl.pallas_call(
        _s5_prompt_kernel,
        out_shape=(jax.ShapeDtypeStruct((bsz, T, sg_n * uw), BF16),
                   jax.ShapeDtypeStruct((bsz, 1, sg_n * sw), F32),
                   jax.ShapeDtypeStruct((bsz, 1, sg_n * sw), F32)),
        grid=(sg_n, bsz, T // tt),
        in_specs=[pl.BlockSpec((tt, uw), lambda g, b, t: (b * (T // tt) + t, u_col0 + g)),
                  per_sg((uw, sw)), per_sg((uw, sw)), per_sg((sw, uw)), per_sg((sw, uw)),
                  vec, per_sg((uw, uw)), vec, per_sg((uw, uw)), vec, pw, pw],
        out_specs=(pl.BlockSpec((None, tt, uw), lambda g, b, t: (b, t, g)),
                   pl.BlockSpec((None, 1, sw), lambda g, b, t: (b, 0, g)),
                   pl.BlockSpec((None, 1, sw), lambda g, b, t: (b, 0, g))),
        scratch_shapes=[pltpu.VMEM((1, sw), F32), pltpu.VMEM((1, sw), F32), pltpu.VMEM((tt, uw), F32)],
        compiler_params=_params(("parallel", "parallel", "arbitrary")),
        name="s5_prompt",
    )(z, tb['bb'][0], tb['bb'][1], tb['cc'][0], tb['cc'][1], tb['d'], tb['w1'], tb['b1'], tb['w2'],
      tb['b2'], tb['pw'][0], tb['pw'][1])
    return o_b, xr, xi


def _s5_sample_kernel(u_ref, x0r_ref, x0i_ref, are_ref, aim_ref, bre_ref, bim_ref, cre_ref, cim_ref, d_ref,
                      w1_ref, b1_ref, w2_ref, b2_ref, o_ref, xr_ref, xi_ref):
    u = u_ref[...]
    ub = u.astype(BF16)
    ar, ai = are_ref[...], aim_ref[...]
    x0r, x0i = x0r_ref[...], x0i_ref[...]
    xr = jnp.dot(ub, bre_ref[...], preferred_element_type=F32) + ar * x0r - ai * x0i
    xi = jnp.dot(ub, bim_ref[...], preferred_element_type=F32) + ar * x0i + ai * x0r
    xr_ref[...] = xr
    xi_ref[...] = xi
    o_ref[...] = _s5_readout(u, xr, xi, cre_ref, cim_ref, d_ref, w1_ref, b1_ref, w2_ref, b2_ref).astype(o_ref.dtype)


def _s5_sample(z_s, x0r, x0i, tb, u_col0):
    nb = z_s.shape[0]
    sg_n, uw, sw = tb['bb'][0].shape
    per_sg = lambda shp: pl.BlockSpec((None,) + shp, lambda g: (g, 0, 0))
    vec = pl.BlockSpec((1, uw), lambda g: (0, g))
    st = pl.BlockSpec((nb, sw), lambda g: (0, g))
    svec = pl.BlockSpec((1, sw), lambda g: (0, g))
    return pl.pallas_call(
        _s5_sample_kernel,
        out_shape=(jax.ShapeDtypeStruct((nb, sg_n * uw), BF16),
                   jax.ShapeDtypeStruct((nb, sg_n * sw), F32),
                   jax.ShapeDtypeStruct((nb, sg_n * sw), F32)),
        grid=(sg_n,),
        in_specs=[pl.BlockSpec((nb, uw), lambda g: (0, u_col0 + g)), st, st, svec, svec,
                  per_sg((uw, sw)), per_sg((uw, sw)), per_sg((sw, uw)), per_sg((sw, uw)),
                  vec, per_sg((uw, uw)), vec, per_sg((uw, uw)), vec],
        out_specs=(pl.BlockSpec((nb, uw), lambda g: (0, g)), st, st),
        compiler_params=_params(("parallel",)),
        name="s5_sample",
    )(z_s, x0r, x0i, tb['abar'][0], tb['abar'][1], tb['bb'][0], tb['bb'][1], tb['cc'][0], tb['cc'][1],
      tb['d'], tb['w1'], tb['b1'], tb['w2'], tb['b2'])


def _pack_bf16_pair(hi, lo):
    bits = lambda a: lax.bitcast_convert_type(a.astype(BF16).astype(F32), jnp.uint32)
    return bits(hi) | (bits(lo) >> 16)


def _unpack_bf16_pair(packed):
    hi = lax.bitcast_convert_type(packed & jnp.uint32(0xFFFF0000), F32).astype(BF16)
    lo = lax.bitcast_convert_type(packed << 16, F32).astype(BF16)
    return hi, lo


def _outproj_kernel(oa_ref, ob_ref, x_ref, w_ref, g_ref, wrh_ref, wrl_ref, br_ref,
                    h1_ref, tok_ref, ti_ref, tw_ref, h1_sc, *, n_experts, row_chunk):
    j = pl.program_id(1)
    nj = pl.num_programs(1)
    tm, d = h1_sc.shape
    tn = w_ref.shape[1]
    half = oa_ref.shape[1]

    h = (x_ref[...] + jnp.dot(oa_ref[...], w_ref[:half, :], preferred_element_type=F32)
         + jnp.dot(ob_ref[...], w_ref[half:, :], preferred_element_type=F32))
    h1_ref[...] = h
    for jj in range(d // tn):
        @pl.when(j == jj)
        def _():
            h1_sc[:, jj * tn:(jj + 1) * tn] = h

    @pl.when(j == nj - 1)
    def _():
        for r0 in range(0, tm, row_chunk):
            rs = slice(r0, r0 + row_chunk)
            h1 = h1_sc[rs, :]
            ms = jnp.mean(h1 * h1, axis=-1, keepdims=True)
            tok = h1 * lax.rsqrt(ms + EPS) * g_ref[...]
            tok_ref[rs, :] = _pack_bf16_pair(tok[:, :d // 2], tok[:, d // 2:])
            t_hi = tok.astype(BF16)
            t_lo = (tok - t_hi.astype(F32)).astype(BF16)
            logits = (jnp.dot(t_hi, wrh_ref[...], preferred_element_type=F32)
                      + jnp.dot(t_hi, wrl_ref[...], preferred_element_type=F32)
                      + jnp.dot(t_lo, wrh_ref[...], preferred_element_type=F32) + br_ref[...])
            lane = lax.broadcasted_iota(jnp.int32, logits.shape, 1).astype(F32)
            l = jnp.where(lane < n_experts, logits, -jnp.inf)
            vals, idxs = [], []
            for _ in range(TOP_K):
                m = jnp.max(l, axis=-1, keepdims=True)
                i = jnp.min(jnp.where(l == m, lane, float(LANES)), axis=-1, keepdims=True)
                vals.append(m)
                idxs.append(i)
                l = jnp.where(lane == i, -jnp.inf, l)
            es = [jnp.exp(vv - vals[0]) for vv in vals]
            tot = es[0] + es[1] + es[2] + es[3]
            ti = jnp.zeros(logits.shape, jnp.int32)
            tw = jnp.zeros(logits.shape, F32)
            for r in range(TOP_K):
                ti = jnp.where(lane == r, idxs[r].astype(jnp.int32), ti)
                tw = jnp.where(lane == r, es[r] / tot, tw)
            ti_ref[rs, :] = ti
            tw_ref[rs, :] = tw


def _out_proj(o_a, o_b, x, w_bf, g_ffn, w_router, b_router, tm=640, tn=512, row_chunk=128):
    n, d = x.shape
    half = o_a.shape[1]
    n_experts = w_router.shape[1]
    wr = jnp.zeros((d, LANES), F32).at[:, :n_experts].set(w_router)
    wr_hi = wr.astype(BF16)
    wr_lo = (wr - wr_hi.astype(F32)).astype(BF16)
    br = jnp.zeros((1, LANES), F32).at[0, :n_experts].set(b_router)
    row = lambda w: pl.BlockSpec((tm, w), lambda i, j: (i, 0))
    return pl.pallas_call(
        functools.partial(_outproj_kernel, n_experts=n_experts, row_chunk=row_chunk),
        out_shape=(jax.ShapeDtypeStruct((n, d), F32), jax.ShapeDtypeStruct((n, d // 2), jnp.uint32),
                   jax.ShapeDtypeStruct((n, LANES), jnp.int32), jax.ShapeDtypeStruct((n, LANES), F32)),
        grid=(n // tm, d // tn),
        in_specs=[row(half), row(half),
                  pl.BlockSpec((tm, tn), lambda i, j: (i, j)),
                  pl.BlockSpec((d, tn), lambda i, j: (0, j)),
                  pl.BlockSpec((1, d), lambda i, j: (0, 0)),
                  pl.BlockSpec((d, LANES), lambda i, j: (0, 0)),
                  pl.BlockSpec((d, LANES), lambda i, j: (0, 0)),
                  pl.BlockSpec((1, LANES), lambda i, j: (0, 0))],
        out_specs=(pl.BlockSpec((tm, tn), lambda i, j: (i, j)), row(d // 2), row(LANES), row(LANES)),
        scratch_shapes=[pltpu.VMEM((tm, d), F32)],
        compiler_params=_params(("parallel", "arbitrary")),
        name="out_proj_router",
    )(o_a, o_b, x, w_bf, g_ffn.reshape(1, d), wr_hi, wr_lo, br)


def _moe_plan(top_i, n_experts):
    n_assign = top_i.size
    flat_e = top_i.reshape(-1)
    onehot = (flat_e[:, None] == jnp.arange(n_experts, dtype=jnp.int32)[None, :]).astype(jnp.int32)
    counts = onehot.sum(axis=0)
    rank = jnp.take_along_axis(jnp.cumsum(onehot, axis=0) - onehot, flat_e[:, None], axis=1)[:, 0]
    nsub_e = (counts + MOE_SUB - 1) // MOE_SUB
    group_start = (jnp.cumsum(nsub_e) - nsub_e) * MOE_SUB
    nitems_e = (nsub_e + MOE_ITEM_SUBS - 1) // MOE_ITEM_SUBS
    item_end = jnp.cumsum(nitems_e)
    item_start = item_end - nitems_e
    n_items = n_assign // MOE_ITEM_ROWS + 1 + n_experts
    m_pad = n_assign + n_experts * MOE_SUB
    w = jnp.arange(n_items, dtype=jnp.int32)
    total = item_end[-1]
    w_eff = jnp.minimum(w, total - 1)
    item_e = jnp.minimum(jnp.sum((item_end[None, :] <= w_eff[:, None]).astype(jnp.int32), axis=1), n_experts - 1)
    local = w_eff - item_start[item_e]
    item_nsub = jnp.where(w < total, jnp.clip(nsub_e[item_e] - local * MOE_ITEM_SUBS, 0, MOE_ITEM_SUBS), 0)
    item_row0 = group_start[item_e] + local * MOE_ITEM_ROWS
    item_slot = w_eff
    cpos = group_start[flat_e] + rank
    ipos = (item_start[flat_e] + rank // MOE_ITEM_ROWS) * MOE_ITEM_ROWS + rank % MOE_ITEM_ROWS
    src_token = (jnp.arange(m_pad, dtype=jnp.int32) % top_i.shape[0]).at[cpos].set(
        jnp.arange(n_assign, dtype=jnp.int32) // top_i.shape[1])
    meta = jnp.stack([item_e, item_row0, item_nsub, item_slot]).astype(jnp.int32)
    ipos_kmajor = ipos.reshape(-1, top_i.shape[1]).T
    return meta, src_token, ipos_kmajor, n_items


def _moe_up_kernel(meta_ref, xs_hbm, wg_ref, wu_ref, bg_ref, bu_ref, act_ref,
                   stage_sc, lhs_sc, wg_sc, wu_sc, sems):
    w, j = pl.program_id(0), pl.program_id(1)
    nsub = meta_ref[2, w]
    half = stage_sc.shape[1]

    def stage_copy(item, s):
        src = xs_hbm.at[pl.ds(pl.multiple_of(meta_ref[1, item] + s * MOE_SUB, MOE_SUB), MOE_SUB), :]
        return pltpu.make_async_copy(src, stage_sc.at[pl.ds(s * MOE_SUB, MOE_SUB), :], sems.at[s])

    def start_item(item):
        for s in range(MOE_ITEM_SUBS):
            @pl.when(s < meta_ref[2, item])
            def _():
                stage_copy(item, s).start()

    @pl.when(j == 0)
    def _():
        @pl.when(w == 0)
        def _():
            start_item(0)

        for s in range(MOE_ITEM_SUBS):
            @pl.when(s < nsub)
            def _():
                stage_copy(w, s).wait()
                rows = slice(s * MOE_SUB, (s + 1) * MOE_SUB)
                hi, lo = _unpack_bf16_pair(stage_sc[rows, :])
                lhs_sc[rows, :half] = hi
                lhs_sc[rows, half:] = lo

        @pl.when(w + 1 < pl.num_programs(0))
        def _():
            start_item(jnp.minimum(w + 1, pl.num_programs(0) - 1))

    @pl.when(nsub > 0)
    def _():
        wg_sc[...] = wg_ref[...].astype(BF16)
        wu_sc[...] = wu_ref[...].astype(BF16)

    for s in range(MOE_ITEM_SUBS):
        rows = slice(s * MOE_SUB, (s + 1) * MOE_SUB)

        @pl.when(s < nsub)
        def _():
            x = lhs_sc[rows, :]
            gate = jnp.dot(x, wg_sc[...], preferred_element_type=F32) + bg_ref[...]
            up = jnp.dot(x, wu_sc[...], preferred_element_type=F32) + bu_ref[...]
            gate = jnp.minimum(gate, SWIGLU_LIMIT)
            up = jnp.clip(up, -SWIGLU_LIMIT, SWIGLU_LIMIT)
            act_ref[rows, :] = ((up + 1.0) * gate * jax.nn.sigmoid(SWIGLU_ALPHA * gate)).astype(BF16)

        @pl.when(jnp.logical_and(s >= nsub, nsub > 0))
        def _():
            act_ref[rows, :] = jnp.zeros((MOE_SUB, act_ref.shape[1]), BF16)


def _moe_up(meta, xs, w_gate_up, b_gate_up, n_items, tn=256):
    n_experts, d, two_ff = w_gate_up.shape
    d_ff = two_ff // 2
    nj = d_ff // tn
    jj = lambda w, j, m: jnp.where(m[2, w] > 0, j, nj - 1)
    grid_spec = pltpu.PrefetchScalarGridSpec(
        num_scalar_prefetch=1,
        grid=(n_items, nj),
        in_specs=[pl.BlockSpec(memory_space=pl.ANY),
                  pl.BlockSpec((None, d, tn), lambda w, j, m: (m[0, w], 0, jj(w, j, m))),
                  pl.BlockSpec((None, d, tn), lambda w, j, m: (m[0, w], 0, nj + jj(w, j, m))),
                  pl.BlockSpec((None, 1, tn), lambda w, j, m: (m[0, w], 0, jj(w, j, m))),
                  pl.BlockSpec((None, 1, tn), lambda w, j, m: (m[0, w], 0, nj + jj(w, j, m)))],
        out_specs=pl.BlockSpec((None, MOE_ITEM_ROWS, tn), lambda w, j, m: (m[3, w], 0, jj(w, j, m))),
        scratch_shapes=[pltpu.VMEM((MOE_ITEM_ROWS, d // 2), jnp.uint32), pltpu.VMEM((MOE_ITEM_ROWS, d), BF16),
                        pltpu.VMEM((d, tn), BF16), pltpu.VMEM((d, tn), BF16),
                        pltpu.SemaphoreType.DMA((MOE_ITEM_SUBS,))])
    bgu = b_gate_up.reshape(n_experts, 1, two_ff)
    return pl.pallas_call(
        _moe_up_kernel,
        out_shape=jax.ShapeDtypeStruct((n_items, MOE_ITEM_ROWS, d_ff), BF16),
        grid_spec=grid_spec,
        compiler_params=_params(("arbitrary", "arbitrary")),
        name="moe_gate_up",
    )(meta, xs, w_gate_up, w_gate_up, bgu, bgu)


def _moe_down_kernel(meta_ref, act_ref, wd_ref, bd_ref, y_ref, wd_sc):
    w = pl.program_id(0)
    nsub = meta_ref[2, w]

    @pl.when(nsub > 0)
    def _():
        wd_sc[...] = wd_ref[...].astype(BF16)

    for s in range(MOE_ITEM_SUBS):
        rows = slice(s * MOE_SUB, (s + 1) * MOE_SUB)

        @pl.when(s < nsub)
        def _():
            y_ref[rows, :] = jnp.dot(act_ref[rows, :], wd_sc[...], preferred_element_type=F32) + bd_ref[...]

        @pl.when(jnp.logical_and(s >= nsub, nsub > 0))
        def _():
            y_ref[rows, :] = jnp.zeros((MOE_SUB, y_ref.shape[1]), F32)


def _moe_down(meta, act, w_down, b_down, tn=512):
    n_experts, d_ff, d = w_down.shape
    n_items = act.shape[0]
    nj = d // tn
    jj = lambda w, j, m: jnp.where(m[2, w] > 0, j, nj - 1)
    grid_spec = pltpu.PrefetchScalarGridSpec(
        num_scalar_prefetch=1,
        grid=(n_items, nj),
        in_specs=[pl.BlockSpec((None, MOE_ITEM_ROWS, d_ff), lambda w, j, m: (m[3, w], 0, 0)),
                  pl.BlockSpec((None, d_ff, tn), lambda w, j, m: (m[0, w], 0, jj(w, j, m))),
                  pl.BlockSpec((None, 1, tn), lambda w, j, m: (m[0, w], 0, jj(w, j, m)))],
        out_specs=pl.BlockSpec((None, MOE_ITEM_ROWS, tn), lambda w, j, m: (m[3, w], 0, jj(w, j, m))),
        scratch_shapes=[pltpu.VMEM((d_ff, tn), BF16)])
    return pl.pallas_call(
        _moe_down_kernel,
        out_shape=jax.ShapeDtypeStruct((n_items, MOE_ITEM_ROWS, d), F32),
        grid_spec=grid_spec,
        compiler_params=_params(("arbitrary", "arbitrary")),
        name="moe_down",
    )(meta, act, w_down, b_down.reshape(n_experts, 1, d))


def _combine_norm_kernel(h1_ref, y0_ref, y1_ref, y2_ref, y3_ref, tw_ref, g_ref, h2_ref, n_ref):
    tw = tw_ref[...]
    h2 = h1_ref[...]
    for k, y_ref in enumerate((y0_ref, y1_ref, y2_ref, y3_ref)):
        h2 = h2 + tw[:, k:k + 1] * y_ref[...]
    ms = jnp.mean(h2 * h2, axis=-1, keepdims=True)
    h2_ref[...] = h2
    n_ref[...] = (h2 * lax.rsqrt(ms + EPS) * g_ref[...]).astype(BF16)


def _combine_norm(h1, y_sel, tw, g_ple, tm=160):
    n, d = h1.shape
    row = pl.BlockSpec((tm, d), lambda i: (i, 0))
    sel = lambda k: pl.BlockSpec((None, tm, d), lambda i: (k, i, 0))
    return pl.pallas_call(
        _combine_norm_kernel,
        out_shape=(jax.ShapeDtypeStruct((n, d), F32), jax.ShapeDtypeStruct((n, d), BF16)),
        grid=(n // tm,),
        in_specs=[row, sel(0), sel(1), sel(2), sel(3), pl.BlockSpec((tm, LANES), lambda i: (i, 0)),
                  pl.BlockSpec((1, d), lambda i: (0, 0))],
        out_specs=(row, row),
        compiler_params=_params(("parallel",)),
        name="moe_combine_norm",
    )(h1, y_sel, y_sel, y_sel, y_sel, tw, g_ple.reshape(1, d))


def _ple_kernel(n_ref, p_ref, h2_ref, wg_ref, wp_ref, gf_ref, y_ref):
    j = pl.program_id(1)
    tn = wg_ref.shape[1]
    gate = jax.nn.sigmoid(jnp.dot(n_ref[...], wg_ref[...], preferred_element_type=F32))
    emb = jnp.dot(p_ref[...].astype(BF16), wp_ref[...], preferred_element_type=F32)
    h3 = h2_ref[...] + gate * emb
    for jj in range(y_ref.shape[1] // tn):
        @pl.when(j == jj)
        def _():
            y_ref[:, jj * tn:(jj + 1) * tn] = h3

    @pl.when(j == pl.num_programs(1) - 1)
    def _():
        for r0 in range(0, y_ref.shape[0], 128):
            h = y_ref[r0:r0 + 128, :]
            ms = jnp.mean(h * h, axis=-1, keepdims=True)
            y_ref[r0:r0 + 128, :] = h * lax.rsqrt(ms + EPS) * gf_ref[...]


def _ple(n3, p, h2, wg_bf, wp_bf, g_final, tm=640, tn=512):
    n, d = h2.shape
    pd = p.shape[1]
    return pl.pallas_call(
        _ple_kernel,
        out_shape=jax.ShapeDtypeStruct((n, d), F32),
        grid=(n // tm, d // tn),
        in_specs=[pl.BlockSpec((tm, d), lambda i, j: (i, 0)),
                  pl.BlockSpec((tm, pd), lambda i, j: (i, 0)),
                  pl.BlockSpec((tm, tn), lambda i, j: (i, j)),
                  pl.BlockSpec((d, tn), lambda i, j: (0, j)),
                  pl.BlockSpec((pd, tn), lambda i, j: (0, j)),
                  pl.BlockSpec((1, d), lambda i, j: (0, 0))],
        out_specs=pl.BlockSpec((tm, d), lambda i, j: (i, 0)),
        compiler_params=_params(("parallel", "arbitrary")),
        name="ple_final",
    )(n3, p, h2, wg_bf, wp_bf, g_final.reshape(1, d))


def kernel(x_prompt, x_sample, state_hgrn, state_s5_re, state_s5_im, p_prompt, p_sample, g_mix, w_in, hgrn_gamma, g_onorm, s5_a_re, s5_a_im, s5_log_dt, s5_b_re, s5_b_im, s5_c_re, s5_c_im, s5_d, glu_w1, glu_b1, glu_w2, glu_b2, w_out, g_ffn, w_router, b_router, w_gate_up, b_gate_up, w_down, b_down, g_ple, w_ple_gate, w_ple_proj, g_final):
    assert w_in.shape[0] == 1, "single-layer trunk"
    bsz, T, d = x_prompt.shape
    nb = x_sample.shape[0]
    n_prompt = bsz * T
    heads = hgrn_gamma.shape[1] // HEAD_DIM
    G, P = s5_a_re.shape[1:]
    n_experts = w_router.shape[-1]
    u_col0 = 4 * heads

    lb = jnp.cumsum(jax.nn.softmax(hgrn_gamma.astype(F32), axis=0), axis=0)[0]
    h = jnp.concatenate([x_prompt.reshape(n_prompt, d), x_sample.reshape(nb, d)], axis=0)
    z = _in_proj(h, g_mix[0], w_in[0].astype(BF16))
    z_s = z[n_prompt:]
    oa_p, s_p = _hgrn_prompt(z, bsz, T, lb, g_onorm[0], heads)
    oa_s, s_s = _hgrn_sample(z_s, state_hgrn[0], lb, g_onorm[0], heads)
    tb = _s5_tables(s5_a_re[0], s5_a_im[0], s5_log_dt[0], s5_b_re[0], s5_b_im[0], s5_c_re[0], s5_c_im[0],
                    s5_d[0], glu_w1[0], glu_b1[0], glu_w2[0], glu_b2[0], S5_TT)
    ob_p, xr_p, xi_p = _s5_prompt(z, bsz, T, tb, u_col0)
    ob_s, xr_s, xi_s = _s5_sample(z_s, state_s5_re[0].reshape(nb, G * P), state_s5_im[0].reshape(nb, G * P),
                                  tb, u_col0)
    o_a = jnp.concatenate([oa_p.reshape(n_prompt, -1), oa_s], axis=0)
    o_b = jnp.concatenate([ob_p.reshape(n_prompt, -1), ob_s], axis=0)
    h1, tok, ti, tw = _out_proj(o_a, o_b, h, w_out[0].astype(BF16), g_ffn[0], w_router[0], b_router[0])
    meta, src_token, ipos, n_items = _moe_plan(ti[:, :TOP_K], n_experts)
    xs = tok.at[src_token].get(mode='promise_in_bounds')
    act = _moe_up(meta, xs, w_gate_up[0], b_gate_up[0], n_items)
    y_items = _moe_down(meta, act, w_down[0], b_down[0])
    y_sel = y_items.reshape(-1, d).at[ipos.reshape(-1)].get(mode='promise_in_bounds')
    h2, n3 = _combine_norm(h1, y_sel.reshape(TOP_K, -1, d), tw, g_ple[0])
    p_all = jnp.concatenate([p_prompt[0].reshape(n_prompt, -1), p_sample[0].reshape(nb, -1)], axis=0)
    y = _ple(n3, p_all, h2, w_ple_gate[0].astype(BF16), w_ple_proj[0].astype(BF16), g_final)
    return (y[:n_prompt].reshape(bsz, T, d), y[n_prompt:].reshape(nb, 1, d),
            s_p[None], xr_p.reshape(1, bsz, G, P), xi_p.reshape(1, bsz, G, P),
            s_s[None], xr_s.reshape(1, nb, G, P), xi_s.reshape(1, nb, G, P))
```

```python
import functools
import math

import jax
import jax.numpy as jnp
from jax import lax
from jax.experimental import pallas as pl
from jax.experimental.pallas import tpu as pltpu

F32 = jnp.float32
BF16 = jnp.bfloat16
HIGHEST = lax.Precision.HIGHEST

EPS = 1e-6
SWIGLU_LIMIT = 7.0
SWIGLU_ALPHA = 1.702
TOP_K = 4

LANES = 128
SUBLANES = 8
VMEM_BYTES_V7X = 64 * 1024 * 1024
VMEM_LIMIT = VMEM_BYTES_V7X - 6 * 1024 * 1024

HEAD_DIM = 128
HGRN_CHUNK = 128
HGRN_SUB = 8
HGRN_STEP_ROWS = 256
S5_PACK = 8
S5_TT = 256

MOE_SUB = 256
MOE_ITEM_SUBS = 6
MOE_ITEM_ROWS = MOE_SUB * MOE_ITEM_SUBS


def _params(sem, vmem=VMEM_LIMIT):
    return pltpu.CompilerParams(dimension_semantics=sem, vmem_limit_bytes=vmem)


def _dot_nt(a, b):
    return lax.dot_general(a, b, (((1,), (1,)), ((), ())), preferred_element_type=F32)


def _dot_tn(a, b):
    return lax.dot_general(a, b, (((0,), (0,)), ((), ())), preferred_element_type=F32)


def _inproj_kernel(x_ref, g_ref, w_ref, z_ref, n_sc):
    @pl.when(pl.program_id(1) == 0)
    def _():
        for r0 in range(0, x_ref.shape[0], 128):
            x = x_ref[r0:r0 + 128, :]
            ms = jnp.mean(x * x, axis=-1, keepdims=True)
            n_sc[r0:r0 + 128, :] = (x * lax.rsqrt(ms + EPS) * g_ref[...]).astype(BF16)

    z_ref[...] = jnp.dot(n_sc[...], w_ref[...], preferred_element_type=F32)


def _in_proj(x, g, w_bf, tm=640, tn=1024):
    n, d = x.shape
    cols = w_bf.shape[1]
    return pl.pallas_call(
        _inproj_kernel,
        out_shape=jax.ShapeDtypeStruct((n, cols), F32),
        grid=(n // tm, cols // tn),
        in_specs=[pl.BlockSpec((tm, d), lambda i, j: (i, 0)),
                  pl.BlockSpec((1, d), lambda i, j: (0, 0)),
                  pl.BlockSpec((d, tn), lambda i, j: (0, j))],
        out_specs=pl.BlockSpec((tm, tn), lambda i, j: (i, j)),
        scratch_shapes=[pltpu.VMEM((tm, d), BF16)],
        compiler_params=_params(("parallel", "arbitrary")),
        name="in_proj",
    )(x, g.reshape(1, d), w_bf)


def _gates(fz, lb):
    f = lb + (1.0 - lb) * jax.nn.sigmoid(fz)
    k = (1.0 - lb) * jax.nn.sigmoid(-fz)
    return f, k


def _onorm_gate(o, gon, gz):
    ms = jnp.mean(o * o, axis=-1, keepdims=True)
    return o * lax.rsqrt(ms + EPS) * gon * (gz * jax.nn.sigmoid(gz))


def _hgrn_prompt_kernel(q_ref, f_ref, v_ref, gz_ref, lb_ref, gon_ref, o_ref, s_ref, st_sc):
    c_idx = pl.program_id(2)
    C, SUB = HGRN_CHUNK, HGRN_SUB

    @pl.when(c_idx == 0)
    def _():
        st_sc[...] = jnp.zeros_like(st_sc)

    lb = lb_ref[...]
    gon = gon_ref[...]
    r = lax.broadcasted_iota(jnp.int32, (C, C), 0)
    c = lax.broadcasted_iota(jnp.int32, (C, C), 1)
    tri = (r >= c).astype(F32)
    rows = lax.broadcasted_iota(jnp.int32, (SUB, HEAD_DIM), 0)
    rfull = lax.broadcasted_iota(jnp.int32, (C, HEAD_DIM), 0)
    levels = []
    m = SUB
    while m < C:
        pair = jnp.logical_and((r // (2 * m)) == (c // (2 * m)),
                               jnp.logical_and((r % (2 * m)) >= m, (c % (2 * m)) < m))
        levels.append((m, pair, (rfull % (2 * m)) >= m))
        m *= 2

    for ci in range(HGRN_STEP_ROWS // C):
        sl = slice(ci * C, (ci + 1) * C)
        q, fz, v, gz = q_ref[sl, :], f_ref[sl, :], v_ref[sl, :], gz_ref[sl, :]
        f, k = _gates(fz, lb)
        b = jnp.dot(tri, jnp.log(f), precision=HIGHEST, preferred_element_type=F32)
        st = st_sc[...]
        o = _dot_nt(q * jnp.exp(b), st)
        att = jnp.zeros((C, C), F32)
        for m, pair, right in levels:
            bs = jnp.concatenate([jnp.broadcast_to(b[r0 + m - 1:r0 + m], (2 * m, HEAD_DIM))
                                  for r0 in range(0, C, 2 * m)], axis=0)
            e = jnp.exp(jnp.where(right, b - bs, bs - b))
            att = att + jnp.where(pair, _dot_nt(q * e, k * e), 0.0)
        o = o + jnp.dot(att, v, preferred_element_type=F32)
        parts = []
        for i in range(C // SUB):
            r0 = i * SUB
            qi, bi, ki, vi = q[r0:r0 + SUB], b[r0:r0 + SUB], k[r0:r0 + SUB], v[r0:r0 + SUB]
            oi = o[r0:r0 + SUB]
            for s in range(SUB):
                e = jnp.exp(jnp.where(rows >= s, bi - bi[s:s + 1], -jnp.inf))
                a_s = jnp.sum(qi * e * ki[s:s + 1], axis=-1, keepdims=True)
                oi = oi + a_s * vi[s:s + 1]
            parts.append(oi)
        o = jnp.concatenate(parts, axis=0)
        bl = b[C - 1:C]
        st = st * jnp.exp(bl) + _dot_tn(v, k * jnp.exp(bl - b))
        st_sc[...] = st
        o_ref[sl, :] = _onorm_gate(o, gon, gz).astype(o_ref.dtype)

    @pl.when(c_idx == pl.num_programs(2) - 1)
    def _():
        s_ref[...] = st_sc[...].T


def _hgrn_prompt(z, bsz, T, lb, g_onorm, heads):
    R = HGRN_STEP_ROWS
    col = lambda off: pl.BlockSpec((R, HEAD_DIM), lambda b, h, c: (b * (T // R) + c, off + h))
    vec = pl.BlockSpec((1, HEAD_DIM), lambda b, h, c: (0, h))
    return pl.pallas_call(
        _hgrn_prompt_kernel,
        out_shape=(jax.ShapeDtypeStruct((bsz, T, heads * HEAD_DIM), BF16),
                   jax.ShapeDtypeStruct((bsz, heads, HEAD_DIM, HEAD_DIM), F32)),
        grid=(bsz, heads, T // R),
        in_specs=[col(0), col(heads), col(2 * heads), col(3 * heads), vec, vec],
        out_specs=(pl.BlockSpec((None, R, HEAD_DIM), lambda b, h, c: (b, c, h)),
                   pl.BlockSpec((None, None, HEAD_DIM, HEAD_DIM), lambda b, h, c: (b, h, 0, 0))),
        scratch_shapes=[pltpu.VMEM((HEAD_DIM, HEAD_DIM), F32)],
        compiler_params=_params(("parallel", "parallel", "arbitrary")),
        name="hgrn_prompt",
    )(z, z, z, z, lb.reshape(1, -1), g_onorm.reshape(1, -1))


def _hgrn_sample_kernel(qT_ref, fT_ref, v_ref, gz_ref, lbT_ref, gon_ref, s0_ref, o_ref, s_ref, o_sc):
    lbT = lbT_ref[...]
    fT, kT = _gates(fT_ref[...], lbT)
    qT = qT_ref[...]
    v = v_ref[...]
    for bb in range(v.shape[0]):
        s_new = fT[:, bb:bb + 1] * s0_ref[bb] + kT[:, bb:bb + 1] * v[bb:bb + 1, :]
        s_ref[bb] = s_new
        o_sc[bb:bb + 1, :] = jnp.sum(qT[:, bb:bb + 1] * s_new, axis=0, keepdims=True)
    o_ref[...] = _onorm_gate(o_sc[...], gon_ref[...], gz_ref[...]).astype(o_ref.dtype)


def _hgrn_sample(z_s, s0, lb, g_onorm, heads):
    nb = bt = z_s.shape[0]
    z_sT = z_s[:, :2 * heads * HEAD_DIM].T
    lbT = jnp.broadcast_to(lb[:, None], (heads * HEAD_DIM, bt))
    return pl.pallas_call(
        _hgrn_sample_kernel,
        out_shape=(jax.ShapeDtypeStruct((nb, heads * HEAD_DIM), BF16),
                   jax.ShapeDtypeStruct(s0.shape, F32)),
        scratch_shapes=[pltpu.VMEM((bt, HEAD_DIM), F32)],
        grid=(heads, nb // bt),
        in_specs=[pl.BlockSpec((HEAD_DIM, bt), lambda h, b: (h, b)),
                  pl.BlockSpec((HEAD_DIM, bt), lambda h, b: (heads + h, b)),
                  pl.BlockSpec((bt, HEAD_DIM), lambda h, b: (b, 2 * heads + h)),
                  pl.BlockSpec((bt, HEAD_DIM), lambda h, b: (b, 3 * heads + h)),
                  pl.BlockSpec((HEAD_DIM, bt), lambda h, b: (h, 0)),
                  pl.BlockSpec((1, HEAD_DIM), lambda h, b: (0, h)),
                  pl.BlockSpec((bt, None, HEAD_DIM, HEAD_DIM), lambda h, b: (b, h, 0, 0))],
        out_specs=(pl.BlockSpec((bt, HEAD_DIM), lambda h, b: (b, h)),
                   pl.BlockSpec((bt, None, HEAD_DIM, HEAD_DIM), lambda h, b: (b, h, 0, 0))),
        compiler_params=_params(("parallel", "parallel")),
        name="hgrn_sample",
    )(z_sT, z_sT, z_s, z_s, lbT, g_onorm.reshape(1, -1), s0)


def _s5_tables(a_re, a_im, log_dt, b_re, b_im, c_re, c_im, d_skip, w1, b1, w2, b2, tt):
    G, P = a_re.shape
    HG = d_skip.shape[-1]
    SG = G // S5_PACK
    dt = jnp.exp(log_dt)
    ar, ai = a_re * dt, a_im * dt
    mag = jnp.exp(ar)
    abar_re, abar_im = mag * jnp.cos(ai), mag * jnp.sin(ai)
    den = a_re * a_re + a_im * a_im
    nr = abar_re - 1.0
    cr = (nr * a_re + abar_im * a_im) / den
    ci = (abar_im * a_re - nr * a_im) / den
    bb_re = cr[..., None] * b_re - ci[..., None] * b_im
    bb_im = cr[..., None] * b_im + ci[..., None] * b_re
    eye = jnp.eye(S5_PACK, dtype=F32)

    def bdiag_in(bb):
        t = jnp.einsum('sgph,gk->sghkp', bb.reshape(SG, S5_PACK, P, HG), eye)
        return t.reshape(SG, S5_PACK * HG, S5_PACK * P).astype(BF16)

    def bdiag_out(cc):
        t = jnp.einsum('sghp,gk->sgpkh', cc.reshape(SG, S5_PACK, HG, P), eye)
        return t.reshape(SG, S5_PACK * P, S5_PACK * HG).astype(BF16)

    def bdiag_sq(w):
        t = jnp.einsum('sghk,gj->sghjk', w.reshape(SG, S5_PACK, HG, HG), eye)
        return t.reshape(SG, S5_PACK * HG, S5_PACK * HG).astype(BF16)

    def powers(steps):
        st = steps.astype(F32)[:, None, None]
        pm = jnp.exp(st * ar)
        return ((pm * jnp.cos(st * ai)).reshape(len(steps), G * P),
                (pm * jnp.sin(st * ai)).reshape(len(steps), G * P))

    pw = powers(jnp.arange(1, tt // SUBLANES + 1))
    return dict(
        abar=(abar_re.reshape(1, G * P), abar_im.reshape(1, G * P)),
        bb=(bdiag_in(bb_re), bdiag_in(bb_im)), cc=(bdiag_out(c_re), bdiag_out(c_im)),
        d=d_skip.reshape(1, G * HG), w1=bdiag_sq(w1), b1=b1.reshape(1, G * HG),
        w2=bdiag_sq(w2), b2=b2.reshape(1, G * HG), pw=pw)


def _s5_readout(u, xr, xi, cre_ref, cim_ref, d_ref, w1_ref, b1_ref, w2_ref, b2_ref):
    y = (jnp.dot(xr.astype(BF16), cre_ref[...], preferred_element_type=F32)
         - jnp.dot(xi.astype(BF16), cim_ref[...], preferred_element_type=F32) + d_ref[...] * u)
    yg = jax.nn.gelu(y).astype(BF16)
    return ((jnp.dot(yg, w1_ref[...], preferred_element_type=F32) + b1_ref[...])
            * jax.nn.sigmoid(jnp.dot(yg, w2_ref[...], preferred_element_type=F32) + b2_ref[...]))


def _cmul_add(pr, pi, xr, xi, ar, ai):
    return pr * xr - pi * xi + ar, pr * xi + pi * xr + ai


def _s5_prompt_kernel(u_ref, bre_ref, bim_ref, cre_ref, cim_ref, d_ref, w1_ref, b1_ref, w2_ref, b2_ref,
                      pwre_ref, pwim_ref, o_ref, xr_ref, xi_ref, car_re, car_im, out_sc):
    t_idx = pl.program_id(2)
    tt = u_ref.shape[0]
    seg = tt // SUBLANES

    @pl.when(t_idx == 0)
    def _():
        car_re[...] = jnp.zeros_like(car_re)
        car_im[...] = jnp.zeros_like(car_im)

    u = jnp.concatenate([u_ref[pl.ds(j, SUBLANES, stride=seg), :] for j in range(seg)], axis=0)
    ub = u.astype(BF16)
    bur = jnp.dot(ub, bre_ref[...], preferred_element_type=F32)
    bui = jnp.dot(ub, bim_ref[...], preferred_element_type=F32)
    grp = lambda a, j: a[j * SUBLANES:(j + 1) * SUBLANES]
    a1r, a1i = pwre_ref[0:1, :], pwim_ref[0:1, :]
    lr, li = [grp(bur, 0)], [grp(bui, 0)]
    for j in range(1, seg):
        r, i = _cmul_add(a1r, a1i, lr[-1], li[-1], grp(bur, j), grp(bui, j))
        lr.append(r)
        li.append(i)
    asr, asi = pwre_ref[seg - 1:seg, :], pwim_ref[seg - 1:seg, :]
    sr, si = car_re[...], car_im[...]
    in_r, in_i = [sr], [si]
    for s in range(1, SUBLANES + 1):
        sr, si = _cmul_add(asr, asi, sr, si, lr[-1][s - 1:s], li[-1][s - 1:s])
        if s < SUBLANES:
            in_r.append(sr)
            in_i.append(si)
    car_re[...] = sr
    car_im[...] = si
    inr, ini = jnp.concatenate(in_r, axis=0), jnp.concatenate(in_i, axis=0)
    xs = [_cmul_add(pwre_ref[j:j + 1, :], pwim_ref[j:j + 1, :], inr, ini, lr[j], li[j]) for j in range(seg)]
    xr = jnp.concatenate([x[0] for x in xs], axis=0)
    xi = jnp.concatenate([x[1] for x in xs], axis=0)
    out = _s5_readout(u, xr, xi, cre_ref, cim_ref, d_ref, w1_ref, b1_ref, w2_ref, b2_ref)
    for j in range(seg):
        out_sc[pl.ds(j, SUBLANES, stride=seg), :] = grp(out, j)
    o_ref[...] = out_sc[...].astype(o_ref.dtype)

    @pl.when(t_idx == pl.num_programs(2) - 1)
    def _():
        xr_ref[...] = sr
        xi_ref[...] = si


def _s5_prompt(z, bsz, T, tb, u_col0):
    sg_n, uw, sw = tb['bb'][0].shape
    seg = tb['pw'][0].shape[0]
    tt = seg * SUBLANES
    per_sg = lambda shp: pl.BlockSpec((None,) + shp, lambda g, b, t: (g, 0, 0))
    vec = pl.BlockSpec((1, uw), lambda g, b, t: (0, g))
    pw = pl.BlockSpec((seg, sw), lambda g, b, t: (0, g))
    o_b, xr, xi = pl.pallas_call(
        _s5_prompt_kernel,
        out_shape=(jax.ShapeDtypeStruct((bsz, T, sg_n * uw), BF16),
                   jax.ShapeDtypeStruct((bsz, 1, sg_n * sw), F32),
                   jax.ShapeDtypeStruct((bsz, 1, sg_n * sw), F32)),
        grid=(sg_n, bsz, T // tt),
        in_specs=[pl.BlockSpec((tt, uw), lambda g, b, t: (b * (T // tt) + t, u_col0 + g)),
                  per_sg((uw, sw)), per_sg((uw, sw)), per_sg((sw, uw)), per_sg((sw, uw)),
                  vec, per_sg((uw, uw)), vec, per_sg((uw, uw)), vec, pw, pw],
        out_specs=(pl.BlockSpec((None, tt, uw), lambda g, b, t: (b, t, g)),
                   pl.BlockSpec((None, 1, sw), lambda g, b, t: (b, 0, g)),
                   pl.BlockSpec((None, 1, sw), lambda g, b, t: (b, 0, g))),
        scratch_shapes=[pltpu.VMEM((1, sw), F32), pltpu.VMEM((1, sw), F32), pltpu.VMEM((tt, uw), F32)],
        compiler_params=_params(("parallel", "parallel", "arbitrary")),
        name="s5_prompt",
    )(z, tb['bb'][0], tb['bb'][1], tb['cc'][0], tb['cc'][1], tb['d'], tb['w1'], tb['b1'], tb['w2'],
      tb['b2'], tb['pw'][0], tb['pw'][1])
    return o_b, xr, xi


def _s5_sample_kernel(u_ref, x0r_ref, x0i_ref, are_ref, aim_ref, bre_ref, bim_ref, cre_ref, cim_ref, d_ref,
                      w1_ref, b1_ref, w2_ref, b2_ref, o_ref, xr_ref, xi_ref):
    u = u_ref[...]
    ub = u.astype(BF16)
    ar, ai = are_ref[...], aim_ref[...]
    x0r, x0i = x0r_ref[...], x0i_ref[...]
    xr = jnp.dot(ub, bre_ref[...], preferred_element_type=F32) + ar * x0r - ai * x0i
    xi = jnp.dot(ub, bim_ref[...], preferred_element_type=F32) + ar * x0i + ai * x0r
    xr_ref[...] = xr
    xi_ref[...] = xi
    o_ref[...] = _s5_readout(u, xr, xi, cre_ref, cim_ref, d_ref, w1_ref, b1_ref, w2_ref, b2_ref).astype(o_ref.dtype)


def _s5_sample(z_s, x0r, x0i, tb, u_col0):
    nb = z_s.shape[0]
    sg_n, uw, sw = tb['bb'][0].shape
    per_sg = lambda shp: pl.BlockSpec((None,) + shp, lambda g: (g, 0, 0))
    vec = pl.BlockSpec((1, uw), lambda g: (0, g))
    st = pl.BlockSpec((nb, sw), lambda g: (0, g))
    svec = pl.BlockSpec((1, sw), lambda g: (0, g))
    return pl.pallas_call(
        _s5_sample_kernel,
        out_shape=(jax.ShapeDtypeStruct((nb, sg_n * uw), BF16),
                   jax.ShapeDtypeStruct((nb, sg_n * sw), F32),
                   jax.ShapeDtypeStruct((nb, sg_n * sw), F32)),
        grid=(sg_n,),
        in_specs=[pl.BlockSpec((nb, uw), lambda g: (0, u_col0 + g)), st, st, svec, svec,
                  per_sg((uw, sw)), per_sg((uw, sw)), per_sg((sw, uw)), per_sg((sw, uw)),
                  vec, per_sg((uw, uw)), vec, per_sg((uw, uw)), vec],
        out_specs=(pl.BlockSpec((nb, uw), lambda g: (0, g)), st, st),
        compiler_params=_params(("parallel",)),
        name="s5_sample",
    )(z_s, x0r, x0i, tb['abar'][0], tb['abar'][1], tb['bb'][0], tb['bb'][1], tb['cc'][0], tb['cc'][1],
      tb['d'], tb['w1'], tb['b1'], tb['w2'], tb['b2'])


def _pack_bf16_pair(hi, lo):
    bits = lambda a: lax.bitcast_convert_type(a.astype(BF16).astype(F32), jnp.uint32)
    return bits(hi) | (bits(lo) >> 16)


def _unpack_bf16_pair(packed):
    hi = lax.bitcast_convert_type(packed & jnp.uint32(0xFFFF0000), F32).astype(BF16)
    lo = lax.bitcast_convert_type(packed << 16, F32).astype(BF16)
    return hi, lo


def _outproj_kernel(oa_ref, ob_ref, x_ref, w_ref, g_ref, wrh_ref, wrl_ref, br_ref,
                    h1_ref, tok_ref, ti_ref, tw_ref, h1_sc, *, n_experts, row_chunk):
    j = pl.program_id(1)
    nj = pl.num_programs(1)
    tm, d = h1_sc.shape
    tn = w_ref.shape[1]
    half = oa_ref.shape[1]

    h = (x_ref[...] + jnp.dot(oa_ref[...], w_ref[:half, :], preferred_element_type=F32)
         + jnp.dot(ob_ref[...], w_ref[half:, :], preferred_element_type=F32))
    h1_ref[...] = h
    for jj in range(d // tn):
        @pl.when(j == jj)
        def _():
            h1_sc[:, jj * tn:(jj + 1) * tn] = h

    @pl.when(j == nj - 1)
    def _():
        for r0 in range(0, tm, row_chunk):
            rs = slice(r0, r0 + row_chunk)
            h1 = h1_sc[rs, :]
            ms = jnp.mean(h1 * h1, axis=-1, keepdims=True)
            tok = h1 * lax.rsqrt(ms + EPS) * g_ref[...]
            tok_ref[rs, :] = _pack_bf16_pair(tok[:, :d // 2], tok[:, d // 2:])
            t_hi = tok.astype(BF16)
            t_lo = (tok - t_hi.astype(F32)).astype(BF16)
            logits = (jnp.dot(t_hi, wrh_ref[...], preferred_element_type=F32)
                      + jnp.dot(t_hi, wrl_ref[...], preferred_element_type=F32)
                      + jnp.dot(t_lo, wrh_ref[...], preferred_element_type=F32) + br_ref[...])
            lane = lax.broadcasted_iota(jnp.int32, logits.shape, 1).astype(F32)
            l = jnp.where(lane < n_experts, logits, -jnp.inf)
            vals, idxs = [], []
            for _ in range(TOP_K):
                m = jnp.max(l, axis=-1, keepdims=True)
                i = jnp.min(jnp.where(l == m, lane, float(LANES)), axis=-1, keepdims=True)
                vals.append(m)
                idxs.append(i)
                l = jnp.where(lane == i, -jnp.inf, l)
            es = [jnp.exp(vv - vals[0]) for vv in vals]
            tot = es[0] + es[1] + es[2] + es[3]
            ti = jnp.zeros(logits.shape, jnp.int32)
            tw = jnp.zeros(logits.shape, F32)
            for r in range(TOP_K):
                ti = jnp.where(lane == r, idxs[r].astype(jnp.int32), ti)
                tw = jnp.where(lane == r, es[r] / tot, tw)
            ti_ref[rs, :] = ti
            tw_ref[rs, :] = tw


def _out_proj(o_a, o_b, x, w_bf, g_ffn, w_router, b_router, tm=640, tn=512, row_chunk=128):
    n, d = x.shape
    half = o_a.shape[1]
    n_experts = w_router.shape[1]
    wr = jnp.zeros((d, LANES), F32).at[:, :n_experts].set(w_router)
    wr_hi = wr.astype(BF16)
    wr_lo = (wr - wr_hi.astype(F32)).astype(BF16)
    br = jnp.zeros((1, LANES), F32).at[0, :n_experts].set(b_router)
    row = lambda w: pl.BlockSpec((tm, w), lambda i, j: (i, 0))
    return pl.pallas_call(
        functools.partial(_outproj_kernel, n_experts=n_experts, row_chunk=row_chunk),
        out_shape=(jax.ShapeDtypeStruct((n, d), F32), jax.ShapeDtypeStruct((n, d // 2), jnp.uint32),
                   jax.ShapeDtypeStruct((n, LANES), jnp.int32), jax.ShapeDtypeStruct((n, LANES), F32)),
        grid=(n // tm, d // tn),
        in_specs=[row(half), row(half),
                  pl.BlockSpec((tm, tn), lambda i, j: (i, j)),
                  pl.BlockSpec((d, tn), lambda i, j: (0, j)),
                  pl.BlockSpec((1, d), lambda i, j: (0, 0)),
                  pl.BlockSpec((d, LANES), lambda i, j: (0, 0)),
                  pl.BlockSpec((d, LANES), lambda i, j: (0, 0)),
                  pl.BlockSpec((1, LANES), lambda i, j: (0, 0))],
        out_specs=(pl.BlockSpec((tm, tn), lambda i, j: (i, j)), row(d // 2), row(LANES), row(LANES)),
        scratch_shapes=[pltpu.VMEM((tm, d), F32)],
        compiler_params=_params(("parallel", "arbitrary")),
        name="out_proj_router",
    )(o_a, o_b, x, w_bf, g_ffn.reshape(1, d), wr_hi, wr_lo, br)


def _moe_plan(top_i, n_experts):
    n_assign = top_i.size
    flat_e = top_i.reshape(-1)
    onehot = (flat_e[:, None] == jnp.arange(n_experts, dtype=jnp.int32)[None, :]).astype(jnp.int32)
    counts = onehot.sum(axis=0)
    rank = jnp.take_along_axis(jnp.cumsum(onehot, axis=0) - onehot, flat_e[:, None], axis=1)[:, 0]
    nsub_e = (counts + MOE_SUB - 1) // MOE_SUB
    group_start = (jnp.cumsum(nsub_e) - nsub_e) * MOE_SUB
    nitems_e = (nsub_e + MOE_ITEM_SUBS - 1) // MOE_ITEM_SUBS
    item_end = jnp.cumsum(nitems_e)
    item_start = item_end - nitems_e
    n_items = n_assign // MOE_ITEM_ROWS + 1 + n_experts
    m_pad = n_assign + n_experts * MOE_SUB
    w = jnp.arange(n_items, dtype=jnp.int32)
    total = item_end[-1]
    w_eff = jnp.minimum(w, total - 1)
    item_e = jnp.minimum(jnp.sum((item_end[None, :] <= w_eff[:, None]).astype(jnp.int32), axis=1), n_experts - 1)
    local = w_eff - item_start[item_e]
    item_nsub = jnp.where(w < total, jnp.clip(nsub_e[item_e] - local * MOE_ITEM_SUBS, 0, MOE_ITEM_SUBS), 0)
    item_row0 = group_start[item_e] + local * MOE_ITEM_ROWS
    item_slot = w_eff
    cpos = group_start[flat_e] + rank
    ipos = (item_start[flat_e] + rank // MOE_ITEM_ROWS) * MOE_ITEM_ROWS + rank % MOE_ITEM_ROWS
    src_token = (jnp.arange(m_pad, dtype=jnp.int32) % top_i.shape[0]).at[cpos].set(
        jnp.arange(n_assign, dtype=jnp.int32) // top_i.shape[1])
    meta = jnp.stack([item_e, item_row0, item_nsub, item_slot]).astype(jnp.int32)
    ipos_kmajor = ipos.reshape(-1, top_i.shape[1]).T
    return meta, src_token, ipos_kmajor, n_items


def _moe_row_chunks(nsub, compute, fill):
    for c in range(0, MOE_ITEM_SUBS, 2):
        lo, mid, hi = c * MOE_SUB, (c + 1) * MOE_SUB, (c + 2) * MOE_SUB

        @pl.when(c + 2 <= nsub)
        def _():
            compute(slice(lo, hi))

        @pl.when(c + 1 == nsub)
        def _():
            compute(slice(lo, mid))
            fill(slice(mid, hi))

        @pl.when(jnp.logical_and(c >= nsub, nsub > 0))
        def _():
            fill(slice(lo, mid))
            fill(slice(mid, hi))


def _moe_up_kernel(meta_ref, xs_hbm, wg_ref, wu_ref, bg_ref, bu_ref, act_ref,
                   lhs_sc, sems):
    w, j = pl.program_id(0), pl.program_id(1)
    nsub = meta_ref[2, w]
    half = lhs_sc.shape[1]

    def lhs_copy(s):
        src = xs_hbm.at[pl.ds(pl.multiple_of(meta_ref[1, w] + s * MOE_SUB, MOE_SUB), MOE_SUB), :]
        return pltpu.make_async_copy(src, lhs_sc.at[pl.ds(s * MOE_SUB, MOE_SUB), :], sems.at[s])

    @pl.when(j == 0)
    def _():
        for s in range(MOE_ITEM_SUBS):
            @pl.when(s < nsub)
            def _():
                lhs_copy(s).start()
        for s in range(MOE_ITEM_SUBS):
            @pl.when(s < nsub)
            def _():
                lhs_copy(s).wait()

    def compute(rows):
        hi, lo = _unpack_bf16_pair(lhs_sc[rows, :])

        def proj(w_ref, b_ref):
            return (jnp.dot(hi, w_ref[:half, :].astype(BF16), preferred_element_type=F32)
                    + jnp.dot(lo, w_ref[half:, :].astype(BF16), preferred_element_type=F32) + b_ref[...])

        gate = proj(wg_ref, bg_ref)
        up = proj(wu_ref, bu_ref)
        gate = jnp.minimum(gate, SWIGLU_LIMIT)
        up = jnp.clip(up, -SWIGLU_LIMIT, SWIGLU_LIMIT)
        act_ref[rows, :] = ((up + 1.0) * gate * jax.nn.sigmoid(SWIGLU_ALPHA * gate)).astype(BF16)

    def fill(rows):
        act_ref[rows, :] = jnp.zeros((MOE_SUB, act_ref.shape[1]), BF16)

    _moe_row_chunks(nsub, compute, fill)


def _moe_up(meta, xs, w_gate_up, b_gate_up, n_items, tn=512):
    n_experts, d, two_ff = w_gate_up.shape
    d_ff = two_ff // 2
    nj = d_ff // tn
    jj = lambda w, j, m: jnp.where(m[2, w] > 0, j, nj - 1)
    grid_spec = pltpu.PrefetchScalarGridSpec(
        num_scalar_prefetch=1,
        grid=(n_items, nj),
        in_specs=[pl.BlockSpec(memory_space=pl.ANY),
                  pl.BlockSpec((None, d, tn), lambda w, j, m: (m[0, w], 0, jj(w, j, m))),
                  pl.BlockSpec((None, d, tn), lambda w, j, m: (m[0, w], 0, nj + jj(w, j, m))),
                  pl.BlockSpec((None, 1, tn), lambda w, j, m: (m[0, w], 0, jj(w, j, m))),
                  pl.BlockSpec((None, 1, tn), lambda w, j, m: (m[0, w], 0, nj + jj(w, j, m)))],
        out_specs=pl.BlockSpec((None, MOE_ITEM_ROWS, tn), lambda w, j, m: (m[3, w], 0, jj(w, j, m))),
        scratch_shapes=[pltpu.VMEM((MOE_ITEM_ROWS, d // 2), jnp.uint32),
                        pltpu.SemaphoreType.DMA((MOE_ITEM_SUBS,))])
    bgu = b_gate_up.reshape(n_experts, 1, two_ff)
    return pl.pallas_call(
        _moe_up_kernel,
        out_shape=jax.ShapeDtypeStruct((n_items, MOE_ITEM_ROWS, d_ff), BF16),
        grid_spec=grid_spec,
        compiler_params=_params(("arbitrary", "arbitrary")),
        name="moe_gate_up",
    )(meta, xs, w_gate_up, w_gate_up, bgu, bgu)


def _moe_down_kernel(meta_ref, act_ref, wd_ref, bd_ref, y_ref):
    w = pl.program_id(0)
    nsub = meta_ref[2, w]

    def compute(rows):
        y_ref[rows, :] = (jnp.dot(act_ref[rows, :], wd_ref[...].astype(BF16), preferred_element_type=F32)
                          + bd_ref[...])

    def fill(rows):
        y_ref[rows, :] = jnp.zeros((MOE_SUB, y_ref.shape[1]), F32)

    _moe_row_chunks(nsub, compute, fill)


def _moe_down(meta, act, w_down, b_down, tn=512):
    n_experts, d_ff, d = w_down.shape
    n_items = act.shape[0]
    nj = d // tn
    jj = lambda w, j, m: jnp.where(m[2, w] > 0, j, nj - 1)
    grid_spec = pltpu.PrefetchScalarGridSpec(
        num_scalar_prefetch=1,
        grid=(n_items, nj),
        in_specs=[pl.BlockSpec((None, MOE_ITEM_ROWS, d_ff), lambda w, j, m: (m[3, w], 0, 0)),
                  pl.BlockSpec((None, d_ff, tn), lambda w, j, m: (m[0, w], 0, jj(w, j, m))),
                  pl.BlockSpec((None, 1, tn), lambda w, j, m: (m[0, w], 0, jj(w, j, m)))],
        out_specs=pl.BlockSpec((None, MOE_ITEM_ROWS, tn), lambda w, j, m: (m[3, w], 0, jj(w, j, m))))
    return pl.pallas_call(
        _moe_down_kernel,
        out_shape=jax.ShapeDtypeStruct((n_items, MOE_ITEM_ROWS, d), F32),
        grid_spec=grid_spec,
        compiler_params=_params(("arbitrary", "arbitrary")),
        name="moe_down",
    )(meta, act, w_down, b_down.reshape(n_experts, 1, d))


def _combine_norm_kernel(h1_ref, y0_ref, y1_ref, y2_ref, y3_ref, tw_ref, g_ref, h2_ref, n_ref):
    tw = tw_ref[...]
    h2 = h1_ref[...]
    for k, y_ref in enumerate((y0_ref, y1_ref, y2_ref, y3_ref)):
        h2 = h2 + tw[:, k:k + 1] * y_ref[...]
    ms = jnp.mean(h2 * h2, axis=-1, keepdims=True)
    h2_ref[...] = h2
    n_ref[...] = (h2 * lax.rsqrt(ms + EPS) * g_ref[...]).astype(BF16)


def _combine_norm(h1, y_sel, tw, g_ple, tm=160):
    n, d = h1.shape
    row = pl.BlockSpec((tm, d), lambda i: (i, 0))
    sel = lambda k: pl.BlockSpec((None, tm, d), lambda i: (k, i, 0))
    return pl.pallas_call(
        _combine_norm_kernel,
        out_shape=(jax.ShapeDtypeStruct((n, d), F32), jax.ShapeDtypeStruct((n, d), BF16)),
        grid=(n // tm,),
        in_specs=[row, sel(0), sel(1), sel(2), sel(3), pl.BlockSpec((tm, LANES), lambda i: (i, 0)),
                  pl.BlockSpec((1, d), lambda i: (0, 0))],
        out_specs=(row, row),
        compiler_params=_params(("parallel",)),
        name="moe_combine_norm",
    )(h1, y_sel, y_sel, y_sel, y_sel, tw, g_ple.reshape(1, d))


def _ple_kernel(n_ref, p_ref, h2_ref, wg_ref, wp_ref, gf_ref, y_ref):
    j = pl.program_id(1)
    tn = wg_ref.shape[1]
    gate = jax.nn.sigmoid(jnp.dot(n_ref[...], wg_ref[...], preferred_element_type=F32))
    emb = jnp.dot(p_ref[...].astype(BF16), wp_ref[...], preferred_element_type=F32)
    h3 = h2_ref[...] + gate * emb
    for jj in range(y_ref.shape[1] // tn):
        @pl.when(j == jj)
        def _():
            y_ref[:, jj * tn:(jj + 1) * tn] = h3

    @pl.when(j == pl.num_programs(1) - 1)
    def _():
        for r0 in range(0, y_ref.shape[0], 128):
            h = y_ref[r0:r0 + 128, :]
            ms = jnp.mean(h * h, axis=-1, keepdims=True)
            y_ref[r0:r0 + 128, :] = h * lax.rsqrt(ms + EPS) * gf_ref[...]


def _ple(n3, p, h2, wg_bf, wp_bf, g_final, tm=640, tn=512):
    n, d = h2.shape
    pd = p.shape[1]
    return pl.pallas_call(
        _ple_kernel,
        out_shape=jax.ShapeDtypeStruct((n, d), F32),
        grid=(n // tm, d // tn),
        in_specs=[pl.BlockSpec((tm, d), lambda i, j: (i, 0)),
                  pl.BlockSpec((tm, pd), lambda i, j: (i, 0)),
                  pl.BlockSpec((tm, tn), lambda i, j: (i, j)),
                  pl.BlockSpec((d, tn), lambda i, j: (0, j)),
                  pl.BlockSpec((pd, tn), lambda i, j: (0, j)),
                  pl.BlockSpec((1, d), lambda i, j: (0, 0))],
        out_specs=pl.BlockSpec((tm, d), lambda i, j: (i, 0)),
        compiler_params=_params(("parallel", "arbitrary")),
        name="ple_final",
    )(n3, p, h2, wg_bf, wp_bf, g_final.reshape(1, d))


def kernel(x_prompt, x_sample, state_hgrn, state_s5_re, state_s5_im, p_prompt, p_sample, g_mix, w_in, hgrn_gamma, g_onorm, s5_a_re, s5_a_im, s5_log_dt, s5_b_re, s5_b_im, s5_c_re, s5_c_im, s5_d, glu_w1, glu_b1, glu_w2, glu_b2, w_out, g_ffn, w_router, b_router, w_gate_up, b_gate_up, w_down, b_down, g_ple, w_ple_gate, w_ple_proj, g_final):
    assert w_in.shape[0] == 1, "single-layer trunk"
    bsz, T, d = x_prompt.shape
    nb = x_sample.shape[0]
    n_prompt = bsz * T
    heads = hgrn_gamma.shape[1] // HEAD_DIM
    G, P = s5_a_re.shape[1:]
    n_experts = w_router.shape[-1]
    u_col0 = 4 * heads

    lb = jnp.cumsum(jax.nn.softmax(hgrn_gamma.astype(F32), axis=0), axis=0)[0]
    h = jnp.concatenate([x_prompt.reshape(n_prompt, d), x_sample.reshape(nb, d)], axis=0)
    z = _in_proj(h, g_mix[0], w_in[0].astype(BF16))
    z_s = z[n_prompt:]
    oa_p, s_p = _hgrn_prompt(z, bsz, T, lb, g_onorm[0], heads)
    oa_s, s_s = _hgrn_sample(z_s, state_hgrn[0], lb, g_onorm[0], heads)
    tb = _s5_tables(s5_a_re[0], s5_a_im[0], s5_log_dt[0], s5_b_re[0], s5_b_im[0], s5_c_re[0], s5_c_im[0],
                    s5_d[0], glu_w1[0], glu_b1[0], glu_w2[0], glu_b2[0], S5_TT)
    ob_p, xr_p, xi_p = _s5_prompt(z, bsz, T, tb, u_col0)
    ob_s, xr_s, xi_s = _s5_sample(z_s, state_s5_re[0].reshape(nb, G * P), state_s5_im[0].reshape(nb, G * P),
                                  tb, u_col0)
    o_a = jnp.concatenate([oa_p.reshape(n_prompt, -1), oa_s], axis=0)
    o_b = jnp.concatenate([ob_p.reshape(n_prompt, -1), ob_s], axis=0)
    h1, tok, ti, tw = _out_proj(o_a, o_b, h, w_out[0].astype(BF16), g_ffn[0], w_router[0], b_router[0])
    meta, src_token, ipos, n_items = _moe_plan(ti[:, :TOP_K], n_experts)
    xs = tok.at[src_token].get(mode='promise_in_bounds')
    act = _moe_up(meta, xs, w_gate_up[0], b_gate_up[0], n_items)
    y_items = _moe_down(meta, act, w_down[0], b_down[0])
    y_sel = y_items.reshape(-1, d).at[ipos.reshape(-1)].get(mode='promise_in_bounds')
    h2, n3 = _combine_norm(h1, y_sel.reshape(TOP_K, -1, d), tw, g_ple[0])
    p_all = jnp.concatenate([p_prompt[0].reshape(n_prompt, -1), p_sample[0].reshape(nb, -1)], axis=0)
    y = _ple(n3, p_all, h2, w_ple_gate[0].astype(BF16), w_ple_proj[0].astype(BF16), g_final)
    return (y[:n_prompt].reshape(bsz, T, d), y[n_prompt:].reshape(nb, 1, d),
            s_p[None], xr_p.reshape(1, bsz, G, P), xi_p.reshape(1, bsz, G, P),
            s_s[None], xr_s.reshape(1, nb, G, P), xi_s.reshape(1, nb, G, P))
```

```python
import functools
import math

import jax
import jax.numpy as jnp
from jax import lax
from jax.experimental import pallas as pl
from jax.experimental.pallas import tpu as pltpu

F32 = jnp.float32
BF16 = jnp.bfloat16
HIGHEST = lax.Precision.HIGHEST

EPS = 1e-6
SWIGLU_LIMIT = 7.0
SWIGLU_ALPHA = 1.702
TOP_K = 4

LANES = 128
SUBLANES = 8
VMEM_BYTES_V7X = 64 * 1024 * 1024
VMEM_LIMIT = VMEM_BYTES_V7X - 6 * 1024 * 1024

HEAD_DIM = 128
HGRN_CHUNK = 128
HGRN_SUB = 8
HGRN_STEP_ROWS = 1024
S5_PACK = 8
S5_TT = 1024

MOE_SUB = 256
MOE_ITEM_SUBS = 6
MOE_ITEM_ROWS = MOE_SUB * MOE_ITEM_SUBS


def _params(sem, vmem=VMEM_LIMIT):
    return pltpu.CompilerParams(dimension_semantics=sem, vmem_limit_bytes=vmem)


def _dot_nt(a, b):
    return lax.dot_general(a, b, (((1,), (1,)), ((), ())), preferred_element_type=F32)


def _dot_tn(a, b):
    return lax.dot_general(a, b, (((0,), (0,)), ((), ())), preferred_element_type=F32)


def _inproj_kernel(x_ref, g_ref, w_ref, z_ref, n_sc):
    @pl.when(pl.program_id(1) == 0)
    def _():
        for r0 in range(0, x_ref.shape[0], 128):
            x = x_ref[r0:r0 + 128, :]
            ms = jnp.mean(x * x, axis=-1, keepdims=True)
            n_sc[r0:r0 + 128, :] = (x * lax.rsqrt(ms + EPS) * g_ref[...]).astype(BF16)

    z_ref[...] = jnp.dot(n_sc[...], w_ref[...], preferred_element_type=F32)


def _in_proj(x, g, w_bf, tm=640, tn=1024):
    n, d = x.shape
    cols = w_bf.shape[1]
    return pl.pallas_call(
        _inproj_kernel,
        out_shape=jax.ShapeDtypeStruct((n, cols), F32),
        grid=(n // tm, cols // tn),
        in_specs=[pl.BlockSpec((tm, d), lambda i, j: (i, 0)),
                  pl.BlockSpec((1, d), lambda i, j: (0, 0)),
                  pl.BlockSpec((d, tn), lambda i, j: (0, j))],
        out_specs=pl.BlockSpec((tm, tn), lambda i, j: (i, j)),
        scratch_shapes=[pltpu.VMEM((tm, d), BF16)],
        compiler_params=_params(("parallel", "arbitrary")),
        name="in_proj",
    )(x, g.reshape(1, d), w_bf)


def _gates(fz, lb):
    f = lb + (1.0 - lb) * jax.nn.sigmoid(fz)
    k = (1.0 - lb) * jax.nn.sigmoid(-fz)
    return f, k


def _onorm_gate(o, gon, gz):
    ms = jnp.mean(o * o, axis=-1, keepdims=True)
    return o * lax.rsqrt(ms + EPS) * gon * (gz * jax.nn.sigmoid(gz))


def _hgrn_prompt_kernel(q_ref, f_ref, v_ref, gz_ref, lb_ref, gon_ref, o_ref, s_ref, st_sc):
    c_idx = pl.program_id(2)
    C, SUB = HGRN_CHUNK, HGRN_SUB

    @pl.when(c_idx == 0)
    def _():
        st_sc[...] = jnp.zeros_like(st_sc)

    lb = lb_ref[...]
    gon = gon_ref[...]
    r = lax.broadcasted_iota(jnp.int32, (C, C), 0)
    c = lax.broadcasted_iota(jnp.int32, (C, C), 1)
    tri = (r >= c).astype(F32)
    rows = lax.broadcasted_iota(jnp.int32, (SUB, HEAD_DIM), 0)
    rfull = lax.broadcasted_iota(jnp.int32, (C, HEAD_DIM), 0)
    levels = []
    m = SUB
    while m < C:
        pair = jnp.logical_and((r // (2 * m)) == (c // (2 * m)),
                               jnp.logical_and((r % (2 * m)) >= m, (c % (2 * m)) < m))
        levels.append((m, pair, (rfull % (2 * m)) >= m))
        m *= 2

    for ci in range(HGRN_STEP_ROWS // C):
        sl = slice(ci * C, (ci + 1) * C)
        q, fz, v, gz = q_ref[sl, :], f_ref[sl, :], v_ref[sl, :], gz_ref[sl, :]
        f, k = _gates(fz, lb)
        b = jnp.dot(tri, jnp.log(f), precision=HIGHEST, preferred_element_type=F32)
        st = st_sc[...]
        o = _dot_nt(q * jnp.exp(b), st)
        att = jnp.zeros((C, C), F32)
        for m, pair, right in levels:
            bs = jnp.concatenate([jnp.broadcast_to(b[r0 + m - 1:r0 + m], (2 * m, HEAD_DIM))
                                  for r0 in range(0, C, 2 * m)], axis=0)
            e = jnp.exp(jnp.where(right, b - bs, bs - b))
            att = att + jnp.where(pair, _dot_nt(q * e, k * e), 0.0)
        o = o + jnp.dot(att, v, preferred_element_type=F32)
        parts = []
        for i in range(C // SUB):
            r0 = i * SUB
            qi, bi, ki, vi = q[r0:r0 + SUB], b[r0:r0 + SUB], k[r0:r0 + SUB], v[r0:r0 + SUB]
            oi = o[r0:r0 + SUB]
            for s in range(SUB):
                e = jnp.exp(jnp.where(rows >= s, bi - bi[s:s + 1], -jnp.inf))
                a_s = jnp.sum(qi * e * ki[s:s + 1], axis=-1, keepdims=True)
                oi = oi + a_s * vi[s:s + 1]
            parts.append(oi)
        o = jnp.concatenate(parts, axis=0)
        bl = b[C - 1:C]
        st = st * jnp.exp(bl) + _dot_tn(v, k * jnp.exp(bl - b))
        st_sc[...] = st
        o_ref[sl, :] = _onorm_gate(o, gon, gz).astype(o_ref.dtype)

    @pl.when(c_idx == pl.num_programs(2) - 1)
    def _():
        s_ref[...] = st_sc[...].T


def _hgrn_prompt(z, bsz, T, lb, g_onorm, heads):
    R = HGRN_STEP_ROWS
    col = lambda off: pl.BlockSpec((R, HEAD_DIM), lambda b, h, c: (b * (T // R) + c, off + h))
    vec = pl.BlockSpec((1, HEAD_DIM), lambda b, h, c: (0, h))
    return pl.pallas_call(
        _hgrn_prompt_kernel,
        out_shape=(jax.ShapeDtypeStruct((bsz, T, heads * HEAD_DIM), BF16),
                   jax.ShapeDtypeStruct((bsz, heads, HEAD_DIM, HEAD_DIM), F32)),
        grid=(bsz, heads, T // R),
        in_specs=[col(0), col(heads), col(2 * heads), col(3 * heads), vec, vec],
        out_specs=(pl.BlockSpec((None, R, HEAD_DIM), lambda b, h, c: (b, c, h)),
                   pl.BlockSpec((None, None, HEAD_DIM, HEAD_DIM), lambda b, h, c: (b, h, 0, 0))),
        scratch_shapes=[pltpu.VMEM((HEAD_DIM, HEAD_DIM), F32)],
        compiler_params=_params(("parallel", "parallel", "arbitrary")),
        name="hgrn_prompt",
    )(z, z, z, z, lb.reshape(1, -1), g_onorm.reshape(1, -1))


def _hgrn_sample_kernel(qT_ref, fT_ref, v_ref, gz_ref, lbT_ref, gon_ref, s0_ref, o_ref, s_ref, o_sc):
    lbT = lbT_ref[...]
    fT, kT = _gates(fT_ref[...], lbT)
    qT = qT_ref[...]
    v = v_ref[...]
    for bb in range(v.shape[0]):
        s_new = fT[:, bb:bb + 1] * s0_ref[bb] + kT[:, bb:bb + 1] * v[bb:bb + 1, :]
        s_ref[bb] = s_new
        o_sc[bb:bb + 1, :] = jnp.sum(qT[:, bb:bb + 1] * s_new, axis=0, keepdims=True)
    o_ref[...] = _onorm_gate(o_sc[...], gon_ref[...], gz_ref[...]).astype(o_ref.dtype)


def _hgrn_sample(z_s, s0, lb, g_onorm, heads):
    nb = bt = z_s.shape[0]
    z_sT = z_s[:, :2 * heads * HEAD_DIM].T
    lbT = jnp.broadcast_to(lb[:, None], (heads * HEAD_DIM, bt))
    return pl.pallas_call(
        _hgrn_sample_kernel,
        out_shape=(jax.ShapeDtypeStruct((nb, heads * HEAD_DIM), BF16),
                   jax.ShapeDtypeStruct(s0.shape, F32)),
        scratch_shapes=[pltpu.VMEM((bt, HEAD_DIM), F32)],
        grid=(heads, nb // bt),
        in_specs=[pl.BlockSpec((HEAD_DIM, bt), lambda h, b: (h, b)),
                  pl.BlockSpec((HEAD_DIM, bt), lambda h, b: (heads + h, b)),
                  pl.BlockSpec((bt, HEAD_DIM), lambda h, b: (b, 2 * heads + h)),
                  pl.BlockSpec((bt, HEAD_DIM), lambda h, b: (b, 3 * heads + h)),
                  pl.BlockSpec((HEAD_DIM, bt), lambda h, b: (h, 0)),
                  pl.BlockSpec((1, HEAD_DIM), lambda h, b: (0, h)),
                  pl.BlockSpec((bt, None, HEAD_DIM, HEAD_DIM), lambda h, b: (b, h, 0, 0))],
        out_specs=(pl.BlockSpec((bt, HEAD_DIM), lambda h, b: (b, h)),
                   pl.BlockSpec((bt, None, HEAD_DIM, HEAD_DIM), lambda h, b: (b, h, 0, 0))),
        compiler_params=_params(("parallel", "parallel")),
        name="hgrn_sample",
    )(z_sT, z_sT, z_s, z_s, lbT, g_onorm.reshape(1, -1), s0)


def _s5_tables(a_re, a_im, log_dt, b_re, b_im, c_re, c_im, d_skip, w1, b1, w2, b2, tt):
    G, P = a_re.shape
    HG = d_skip.shape[-1]
    SG = G // S5_PACK
    dt = jnp.exp(log_dt)
    ar, ai = a_re * dt, a_im * dt
    mag = jnp.exp(ar)
    abar_re, abar_im = mag * jnp.cos(ai), mag * jnp.sin(ai)
    den = a_re * a_re + a_im * a_im
    nr = abar_re - 1.0
    cr = (nr * a_re + abar_im * a_im) / den
    ci = (abar_im * a_re - nr * a_im) / den
    bb_re = cr[..., None] * b_re - ci[..., None] * b_im
    bb_im = cr[..., None] * b_im + ci[..., None] * b_re
    eye = jnp.eye(S5_PACK, dtype=F32)

    def bdiag_in(bb):
        t = jnp.einsum('sgph,gk->sghkp', bb.reshape(SG, S5_PACK, P, HG), eye)
        return t.reshape(SG, S5_PACK * HG, S5_PACK * P).astype(BF16)

    def bdiag_out(cc):
        t = jnp.einsum('sghp,gk->sgpkh', cc.reshape(SG, S5_PACK, HG, P), eye)
        return t.reshape(SG, S5_PACK * P, S5_PACK * HG).astype(BF16)

    def bdiag_sq(w):
        t = jnp.einsum('sghk,gj->sghjk', w.reshape(SG, S5_PACK, HG, HG), eye)
        return t.reshape(SG, S5_PACK * HG, S5_PACK * HG).astype(BF16)

    def powers(steps):
        st = steps.astype(F32)[:, None, None]
        pm = jnp.exp(st * ar)
        return ((pm * jnp.cos(st * ai)).reshape(len(steps), G * P),
                (pm * jnp.sin(st * ai)).reshape(len(steps), G * P))

    pw = powers(jnp.arange(1, tt // SUBLANES + 1))
    return dict(
        abar=(abar_re.reshape(1, G * P), abar_im.reshape(1, G * P)),
        bb=(bdiag_in(bb_re), bdiag_in(bb_im)), cc=(bdiag_out(c_re), bdiag_out(c_im)),
        d=d_skip.reshape(1, G * HG), w1=bdiag_sq(w1), b1=b1.reshape(1, G * HG),
        w2=bdiag_sq(w2), b2=b2.reshape(1, G * HG), pw=pw)


def _s5_readout(u, xr, xi, cre_ref, cim_ref, d_ref, w1_ref, b1_ref, w2_ref, b2_ref):
    y = (jnp.dot(xr.astype(BF16), cre_ref[...], preferred_element_type=F32)
         - jnp.dot(xi.astype(BF16), cim_ref[...], preferred_element_type=F32) + d_ref[...] * u)
    yg = jax.nn.gelu(y).astype(BF16)
    return ((jnp.dot(yg, w1_ref[...], preferred_element_type=F32) + b1_ref[...])
            * jax.nn.sigmoid(jnp.dot(yg, w2_ref[...], preferred_element_type=F32) + b2_ref[...]))


def _cmul_add(pr, pi, xr, xi, ar, ai):
    return pr * xr - pi * xi + ar, pr * xi + pi * xr + ai


def _s5_prompt_kernel(u_ref, bre_ref, bim_ref, cre_ref, cim_ref, d_ref, w1_ref, b1_ref, w2_ref, b2_ref,
                      pwre_ref, pwim_ref, o_ref, xr_ref, xi_ref, car_re, car_im, out_sc):
    t_idx = pl.program_id(2)
    tt = u_ref.shape[0]
    seg = tt // SUBLANES

    @pl.when(t_idx == 0)
    def _():
        car_re[...] = jnp.zeros_like(car_re)
        car_im[...] = jnp.zeros_like(car_im)

    u = jnp.concatenate([u_ref[pl.ds(j, SUBLANES, stride=seg), :] for j in range(seg)], axis=0)
    ub = u.astype(BF16)
    bur = jnp.dot(ub, bre_ref[...], preferred_element_type=F32)
    bui = jnp.dot(ub, bim_ref[...], preferred_element_type=F32)
    grp = lambda a, j: a[j * SUBLANES:(j + 1) * SUBLANES]
    a1r, a1i = pwre_ref[0:1, :], pwim_ref[0:1, :]
    lr, li = [grp(bur, 0)], [grp(bui, 0)]
    for j in range(1, seg):
        r, i = _cmul_add(a1r, a1i, lr[-1], li[-1], grp(bur, j), grp(bui, j))
        lr.append(r)
        li.append(i)
    asr, asi = pwre_ref[seg - 1:seg, :], pwim_ref[seg - 1:seg, :]
    sr, si = car_re[...], car_im[...]
    in_r, in_i = [sr], [si]
    for s in range(1, SUBLANES + 1):
        sr, si = _cmul_add(asr, asi, sr, si, lr[-1][s - 1:s], li[-1][s - 1:s])
        if s < SUBLANES:
            in_r.append(sr)
            in_i.append(si)
    car_re[...] = sr
    car_im[...] = si
    inr, ini = jnp.concatenate(in_r, axis=0), jnp.concatenate(in_i, axis=0)
    xs = [_cmul_add(pwre_ref[j:j + 1, :], pwim_ref[j:j + 1, :], inr, ini, lr[j], li[j]) for j in range(seg)]
    xr = jnp.concatenate([x[0] for x in xs], axis=0)
    xi = jnp.concatenate([x[1] for x in xs], axis=0)
    out = _s5_readout(u, xr, xi, cre_ref, cim_ref, d_ref, w1_ref, b1_ref, w2_ref, b2_ref)
    for j in range(seg):
        out_sc[pl.ds(j, SUBLANES, stride=seg), :] = grp(out, j)
    o_ref[...] = out_sc[...].astype(o_ref.dtype)

    @pl.when(t_idx == pl.num_programs(2) - 1)
    def _():
        xr_ref[...] = sr
        xi_ref[...] = si


def _s5_prompt(z, bsz, T, tb, u_col0):
    sg_n, uw, sw = tb['bb'][0].shape
    seg = tb['pw'][0].shape[0]
    tt = seg * SUBLANES
    per_sg = lambda shp: pl.BlockSpec((None,) + shp, lambda g, b, t: (g, 0, 0))
    vec = pl.BlockSpec((1, uw), lambda g, b, t: (0, g))
    pw = pl.BlockSpec((seg, sw), lambda g, b, t: (0, g))
    o_b, xr, xi = pl.pallas_call(
        _s5_prompt_kernel,
        out_shape=(jax.ShapeDtypeStruct((bsz, T, sg_n * uw), BF16),
                   jax.ShapeDtypeStruct((bsz, 1, sg_n * sw), F32),
                   jax.ShapeDtypeStruct((bsz, 1, sg_n * sw), F32)),
        grid=(sg_n, bsz, T // tt),
        in_specs=[pl.BlockSpec((tt, uw), lambda g, b, t: (b * (T // tt) + t, u_col0 + g)),
                  per_sg((uw, sw)), per_sg((uw, sw)), per_sg((sw, uw)), per_sg((sw, uw)),
                  vec, per_sg((uw, uw)), vec, per_sg((uw, uw)), vec, pw, pw],
        out_specs=(pl.BlockSpec((None, tt, uw), lambda g, b, t: (b, t, g)),
                   pl.BlockSpec((None, 1, sw), lambda g, b, t: (b, 0, g)),
                   pl.BlockSpec((None, 1, sw), lambda g, b, t: (b, 0, g))),
        scratch_shapes=[pltpu.VMEM((1, sw), F32), pltpu.VMEM((1, sw), F32), pltpu.VMEM((tt, uw), F32)],
        compiler_params=_params(("parallel", "parallel", "arbitrary")),
        name="s5_prompt",
    )(z, tb['bb'][0], tb['bb'][1], tb['cc'][0], tb['cc'][1], tb['d'], tb['w1'], tb['b1'], tb['w2'],
      tb['b2'], tb['pw'][0], tb['pw'][1])
    return o_b, xr, xi


def _s5_sample_kernel(u_ref, x0r_ref, x0i_ref, are_ref, aim_ref, bre_ref, bim_ref, cre_ref, cim_ref, d_ref,
                      w1_ref, b1_ref, w2_ref, b2_ref, o_ref, xr_ref, xi_ref):
    u = u_ref[...]
    ub = u.astype(BF16)
    ar, ai = are_ref[...], aim_ref[...]
    x0r, x0i = x0r_ref[...], x0i_ref[...]
    xr = jnp.dot(ub, bre_ref[...], preferred_element_type=F32) + ar * x0r - ai * x0i
    xi = jnp.dot(ub, bim_ref[...], preferred_element_type=F32) + ar * x0i + ai * x0r
    xr_ref[...] = xr
    xi_ref[...] = xi
    o_ref[...] = _s5_readout(u, xr, xi, cre_ref, cim_ref, d_ref, w1_ref, b1_ref, w2_ref, b2_ref).astype(o_ref.dtype)


def _s5_sample(z_s, x0r, x0i, tb, u_col0):
    nb = z_s.shape[0]
    sg_n, uw, sw = tb['bb'][0].shape
    per_sg = lambda shp: pl.BlockSpec((None,) + shp, lambda g: (g, 0, 0))
    vec = pl.BlockSpec((1, uw), lambda g: (0, g))
    st = pl.BlockSpec((nb, sw), lambda g: (0, g))
    svec = pl.BlockSpec((1, sw), lambda g: (0, g))
    return pl.pallas_call(
        _s5_sample_kernel,
        out_shape=(jax.ShapeDtypeStruct((nb, sg_n * uw), BF16),
                   jax.ShapeDtypeStruct((nb, sg_n * sw), F32),
                   jax.ShapeDtypeStruct((nb, sg_n * sw), F32)),
        grid=(sg_n,),
        in_specs=[pl.BlockSpec((nb, uw), lambda g: (0, u_col0 + g)), st, st, svec, svec,
                  per_sg((uw, sw)), per_sg((uw, sw)), per_sg((sw, uw)), per_sg((sw, uw)),
                  vec, per_sg((uw, uw)), vec, per_sg((uw, uw)), vec],
        out_specs=(pl.BlockSpec((nb, uw), lambda g: (0, g)), st, st),
        compiler_params=_params(("parallel",)),
        name="s5_sample",
    )(z_s, x0r, x0i, tb['abar'][0], tb['abar'][1], tb['bb'][0], tb['bb'][1], tb['cc'][0], tb['cc'][1],
      tb['d'], tb['w1'], tb['b1'], tb['w2'], tb['b2'])


def _pack_bf16_pair(hi, lo):
    bits = lambda a: lax.bitcast_convert_type(a.astype(BF16).astype(F32), jnp.uint32)
    return bits(hi) | (bits(lo) >> 16)


def _unpack_bf16_pair(packed):
    hi = lax.bitcast_convert_type(packed & jnp.uint32(0xFFFF0000), F32).astype(BF16)
    lo = lax.bitcast_convert_type(packed << 16, F32).astype(BF16)
    return hi, lo


def _outproj_kernel(oa_ref, ob_ref, x_ref, w_ref, g_ref, wrh_ref, wrl_ref, br_ref,
                    h1_ref, tok_ref, ti_ref, tw_ref, h1_sc, *, n_experts, row_chunk):
    j = pl.program_id(1)
    nj = pl.num_programs(1)
    tm, d = h1_sc.shape
    tn = w_ref.shape[1]
    half = oa_ref.shape[1]

    h = (x_ref[...] + jnp.dot(oa_ref[...], w_ref[:half, :], preferred_element_type=F32)
         + jnp.dot(ob_ref[...], w_ref[half:, :], preferred_element_type=F32))
    h1_ref[...] = h
    for jj in range(d // tn):
        @pl.when(j == jj)
        def _():
            h1_sc[:, jj * tn:(jj + 1) * tn] = h

    @pl.when(j == nj - 1)
    def _():
        for r0 in range(0, tm, row_chunk):
            rs = slice(r0, r0 + row_chunk)
            h1 = h1_sc[rs, :]
            ms = jnp.mean(h1 * h1, axis=-1, keepdims=True)
            tok = h1 * lax.rsqrt(ms + EPS) * g_ref[...]
            tok_ref[rs, :] = _pack_bf16_pair(tok[:, :d // 2], tok[:, d // 2:])
            t_hi = tok.astype(BF16)
            t_lo = (tok - t_hi.astype(F32)).astype(BF16)
            logits = (jnp.dot(t_hi, wrh_ref[...], preferred_element_type=F32)
                      + jnp.dot(t_hi, wrl_ref[...], preferred_element_type=F32)
                      + jnp.dot(t_lo, wrh_ref[...], preferred_element_type=F32) + br_ref[...])
            lane = lax.broadcasted_iota(jnp.int32, logits.shape, 1).astype(F32)
            l = jnp.where(lane < n_experts, logits, -jnp.inf)
            vals, idxs = [], []
            for _ in range(TOP_K):
                m = jnp.max(l, axis=-1, keepdims=True)
                i = jnp.min(jnp.where(l == m, lane, float(LANES)), axis=-1, keepdims=True)
                vals.append(m)
                idxs.append(i)
                l = jnp.where(lane == i, -jnp.inf, l)
            es = [jnp.exp(vv - vals[0]) for vv in vals]
            tot = es[0] + es[1] + es[2] + es[3]
            ti = jnp.zeros(logits.shape, jnp.int32)
            tw = jnp.zeros(logits.shape, F32)
            for r in range(TOP_K):
                ti = jnp.where(lane == r, idxs[r].astype(jnp.int32), ti)
                tw = jnp.where(lane == r, es[r] / tot, tw)
            ti_ref[rs, :] = ti
            tw_ref[rs, :] = tw


def _out_proj(o_a, o_b, x, w_bf, g_ffn, w_router, b_router, tm=640, tn=512, row_chunk=128):
    n, d = x.shape
    half = o_a.shape[1]
    n_experts = w_router.shape[1]
    wr = jnp.zeros((d, LANES), F32).at[:, :n_experts].set(w_router)
    wr_hi = wr.astype(BF16)
    wr_lo = (wr - wr_hi.astype(F32)).astype(BF16)
    br = jnp.zeros((1, LANES), F32).at[0, :n_experts].set(b_router)
    row = lambda w: pl.BlockSpec((tm, w), lambda i, j: (i, 0))
    return pl.pallas_call(
        functools.partial(_outproj_kernel, n_experts=n_experts, row_chunk=row_chunk),
        out_shape=(jax.ShapeDtypeStruct((n, d), F32), jax.ShapeDtypeStruct((n, d // 2), jnp.uint32),
                   jax.ShapeDtypeStruct((n, LANES), jnp.int32), jax.ShapeDtypeStruct((n, LANES), F32)),
        grid=(n // tm, d // tn),
        in_specs=[row(half), row(half),
                  pl.BlockSpec((tm, tn), lambda i, j: (i, j)),
                  pl.BlockSpec((d, tn), lambda i, j: (0, j)),
                  pl.BlockSpec((1, d), lambda i, j: (0, 0)),
                  pl.BlockSpec((d, LANES), lambda i, j: (0, 0)),
                  pl.BlockSpec((d, LANES), lambda i, j: (0, 0)),
                  pl.BlockSpec((1, LANES), lambda i, j: (0, 0))],
        out_specs=(pl.BlockSpec((tm, tn), lambda i, j: (i, j)), row(d // 2), row(LANES), row(LANES)),
        scratch_shapes=[pltpu.VMEM((tm, d), F32)],
        compiler_params=_params(("parallel", "arbitrary")),
        name="out_proj_router",
    )(o_a, o_b, x, w_bf, g_ffn.reshape(1, d), wr_hi, wr_lo, br)


def _moe_plan(top_i, n_experts):
    n_assign = top_i.size
    flat_e = top_i.reshape(-1)
    onehot = (flat_e[:, None] == jnp.arange(n_experts, dtype=jnp.int32)[None, :]).astype(jnp.int32)
    counts = onehot.sum(axis=0)
    rank = jnp.take_along_axis(jnp.cumsum(onehot, axis=0) - onehot, flat_e[:, None], axis=1)[:, 0]
    nsub_e = (counts + MOE_SUB - 1) // MOE_SUB
    group_start = (jnp.cumsum(nsub_e) - nsub_e) * MOE_SUB
    nitems_e = (nsub_e + MOE_ITEM_SUBS - 1) // MOE_ITEM_SUBS
    item_end = jnp.cumsum(nitems_e)
    item_start = item_end - nitems_e
    n_items = n_assign // MOE_ITEM_ROWS + 1 + n_experts
    m_pad = n_assign + n_experts * MOE_SUB
    w = jnp.arange(n_items, dtype=jnp.int32)
    total = item_end[-1]
    w_eff = jnp.minimum(w, total - 1)
    item_e = jnp.minimum(jnp.sum((item_end[None, :] <= w_eff[:, None]).astype(jnp.int32), axis=1), n_experts - 1)
    local = w_eff - item_start[item_e]
    item_nsub = jnp.where(w < total, jnp.clip(nsub_e[item_e] - local * MOE_ITEM_SUBS, 0, MOE_ITEM_SUBS), 0)
    item_row0 = group_start[item_e] + local * MOE_ITEM_ROWS
    item_slot = w_eff
    cpos = group_start[flat_e] + rank
    ipos = (item_start[flat_e] + rank // MOE_ITEM_ROWS) * MOE_ITEM_ROWS + rank % MOE_ITEM_ROWS
    src_token = (jnp.arange(m_pad, dtype=jnp.int32) % top_i.shape[0]).at[cpos].set(
        jnp.arange(n_assign, dtype=jnp.int32) // top_i.shape[1])
    meta = jnp.stack([item_e, item_row0, item_nsub, item_slot]).astype(jnp.int32)
    ipos_kmajor = ipos.reshape(-1, top_i.shape[1]).T
    return meta, src_token, ipos_kmajor, n_items


def _moe_row_chunks(nsub, compute, fill):
    for c in range(0, MOE_ITEM_SUBS, 2):
        lo, mid, hi = c * MOE_SUB, (c + 1) * MOE_SUB, (c + 2) * MOE_SUB

        @pl.when(c + 2 <= nsub)
        def _():
            compute(slice(lo, hi))

        @pl.when(c + 1 == nsub)
        def _():
            compute(slice(lo, mid))
            fill(slice(mid, hi))

        @pl.when(jnp.logical_and(c >= nsub, nsub > 0))
        def _():
            fill(slice(lo, mid))
            fill(slice(mid, hi))


def _moe_up_kernel(meta_ref, xs_hbm, wg_ref, wu_ref, bg_ref, bu_ref, act_ref,
                   lhs_sc, sems):
    w, j = pl.program_id(0), pl.program_id(1)
    nsub = meta_ref[2, w]
    half = lhs_sc.shape[1]

    def lhs_copy(s):
        src = xs_hbm.at[pl.ds(pl.multiple_of(meta_ref[1, w] + s * MOE_SUB, MOE_SUB), MOE_SUB), :]
        return pltpu.make_async_copy(src, lhs_sc.at[pl.ds(s * MOE_SUB, MOE_SUB), :], sems.at[s])

    @pl.when(j == 0)
    def _():
        for s in range(MOE_ITEM_SUBS):
            @pl.when(s < nsub)
            def _():
                lhs_copy(s).start()
        for s in range(MOE_ITEM_SUBS):
            @pl.when(s < nsub)
            def _():
                lhs_copy(s).wait()

    def compute(rows):
        hi, lo = _unpack_bf16_pair(lhs_sc[rows, :])

        def proj(w_ref, b_ref):
            return (jnp.dot(hi, w_ref[:half, :].astype(BF16), preferred_element_type=F32)
                    + jnp.dot(lo, w_ref[half:, :].astype(BF16), preferred_element_type=F32) + b_ref[...])

        gate = proj(wg_ref, bg_ref)
        up = proj(wu_ref, bu_ref)
        gate = jnp.minimum(gate, SWIGLU_LIMIT)
        up = jnp.clip(up, -SWIGLU_LIMIT, SWIGLU_LIMIT)
        act_ref[rows, :] = ((up + 1.0) * gate * jax.nn.sigmoid(SWIGLU_ALPHA * gate)).astype(BF16)

    def fill(rows):
        act_ref[rows, :] = jnp.zeros((MOE_SUB, act_ref.shape[1]), BF16)

    _moe_row_chunks(nsub, compute, fill)


def _moe_up(meta, xs, w_gate_up, b_gate_up, n_items, tn=512):
    n_experts, d, two_ff = w_gate_up.shape
    d_ff = two_ff // 2
    nj = d_ff // tn
    jj = lambda w, j, m: jnp.where(m[2, w] > 0, j, nj - 1)
    grid_spec = pltpu.PrefetchScalarGridSpec(
        num_scalar_prefetch=1,
        grid=(n_items, nj),
        in_specs=[pl.BlockSpec(memory_space=pl.ANY),
                  pl.BlockSpec((None, d, tn), lambda w, j, m: (m[0, w], 0, jj(w, j, m))),
                  pl.BlockSpec((None, d, tn), lambda w, j, m: (m[0, w], 0, nj + jj(w, j, m))),
                  pl.BlockSpec((None, 1, tn), lambda w, j, m: (m[0, w], 0, jj(w, j, m))),
                  pl.BlockSpec((None, 1, tn), lambda w, j, m: (m[0, w], 0, nj + jj(w, j, m)))],
        out_specs=pl.BlockSpec((None, MOE_ITEM_ROWS, tn), lambda w, j, m: (m[3, w], 0, jj(w, j, m))),
        scratch_shapes=[pltpu.VMEM((MOE_ITEM_ROWS, d // 2), jnp.uint32),
                        pltpu.SemaphoreType.DMA((MOE_ITEM_SUBS,))])
    bgu = b_gate_up.reshape(n_experts, 1, two_ff)
    return pl.pallas_call(
        _moe_up_kernel,
        out_shape=jax.ShapeDtypeStruct((n_items, MOE_ITEM_ROWS, d_ff), BF16),
        grid_spec=grid_spec,
        compiler_params=_params(("arbitrary", "arbitrary")),
        name="moe_gate_up",
    )(meta, xs, w_gate_up, w_gate_up, bgu, bgu)


def _moe_down_kernel(meta_ref, act_ref, wd_ref, bd_ref, y_ref):
    w = pl.program_id(0)
    nsub = meta_ref[2, w]

    def compute(rows):
        y_ref[rows, :] = (jnp.dot(act_ref[rows, :], wd_ref[...].astype(BF16), preferred_element_type=F32)
                          + bd_ref[...])

    def fill(rows):
        y_ref[rows, :] = jnp.zeros((MOE_SUB, y_ref.shape[1]), F32)

    _moe_row_chunks(nsub, compute, fill)


def _moe_down(meta, act, w_down, b_down, tn=512):
    n_experts, d_ff, d = w_down.shape
    n_items = act.shape[0]
    nj = d // tn
    jj = lambda w, j, m: jnp.where(m[2, w] > 0, j, nj - 1)
    grid_spec = pltpu.PrefetchScalarGridSpec(
        num_scalar_prefetch=1,
        grid=(n_items, nj),
        in_specs=[pl.BlockSpec((None, MOE_ITEM_ROWS, d_ff), lambda w, j, m: (m[3, w], 0, 0)),
                  pl.BlockSpec((None, d_ff, tn), lambda w, j, m: (m[0, w], 0, jj(w, j, m))),
                  pl.BlockSpec((None, 1, tn), lambda w, j, m: (m[0, w], 0, jj(w, j, m)))],
        out_specs=pl.BlockSpec((None, MOE_ITEM_ROWS, tn), lambda w, j, m: (m[3, w], 0, jj(w, j, m))))
    return pl.pallas_call(
        _moe_down_kernel,
        out_shape=jax.ShapeDtypeStruct((n_items, MOE_ITEM_ROWS, d), F32),
        grid_spec=grid_spec,
        compiler_params=_params(("arbitrary", "arbitrary")),
        name="moe_down",
    )(meta, act, w_down, b_down.reshape(n_experts, 1, d))


def _combine_norm_kernel(h1_ref, y0_ref, y1_ref, y2_ref, y3_ref, tw_ref, g_ref, h2_ref, n_ref):
    tw = tw_ref[...]
    h2 = h1_ref[...]
    for k, y_ref in enumerate((y0_ref, y1_ref, y2_ref, y3_ref)):
        h2 = h2 + tw[:, k:k + 1] * y_ref[...]
    ms = jnp.mean(h2 * h2, axis=-1, keepdims=True)
    h2_ref[...] = h2
    n_ref[...] = (h2 * lax.rsqrt(ms + EPS) * g_ref[...]).astype(BF16)


def _combine_norm(h1, y_sel, tw, g_ple, tm=160):
    n, d = h1.shape
    row = pl.BlockSpec((tm, d), lambda i: (i, 0))
    sel = lambda k: pl.BlockSpec((None, tm, d), lambda i: (k, i, 0))
    return pl.pallas_call(
        _combine_norm_kernel,
        out_shape=(jax.ShapeDtypeStruct((n, d), F32), jax.ShapeDtypeStruct((n, d), BF16)),
        grid=(n // tm,),
        in_specs=[row, sel(0), sel(1), sel(2), sel(3), pl.BlockSpec((tm, LANES), lambda i: (i, 0)),
                  pl.BlockSpec((1, d), lambda i: (0, 0))],
        out_specs=(row, row),
        compiler_params=_params(("parallel",)),
        name="moe_combine_norm",
    )(h1, y_sel, y_sel, y_sel, y_sel, tw, g_ple.reshape(1, d))


def _ple_kernel(n_ref, p_ref, h2_ref, wg_ref, wp_ref, gf_ref, y_ref):
    j = pl.program_id(1)
    tn = wg_ref.shape[1]
    gate = jax.nn.sigmoid(jnp.dot(n_ref[...], wg_ref[...], preferred_element_type=F32))
    emb = jnp.dot(p_ref[...].astype(BF16), wp_ref[...], preferred_element_type=F32)
    h3 = h2_ref[...] + gate * emb
    for jj in range(y_ref.shape[1] // tn):
        @pl.when(j == jj)
        def _():
            y_ref[:, jj * tn:(jj + 1) * tn] = h3

    @pl.when(j == pl.num_programs(1) - 1)
    def _():
        for r0 in range(0, y_ref.shape[0], 128):
            h = y_ref[r0:r0 + 128, :]
            ms = jnp.mean(h * h, axis=-1, keepdims=True)
            y_ref[r0:r0 + 128, :] = h * lax.rsqrt(ms + EPS) * gf_ref[...]


def _ple(n3, p, h2, wg_bf, wp_bf, g_final, tm=640, tn=512):
    n, d = h2.shape
    pd = p.shape[1]
    return pl.pallas_call(
        _ple_kernel,
        out_shape=jax.ShapeDtypeStruct((n, d), F32),
        grid=(n // tm, d // tn),
        in_specs=[pl.BlockSpec((tm, d), lambda i, j: (i, 0)),
                  pl.BlockSpec((tm, pd), lambda i, j: (i, 0)),
                  pl.BlockSpec((tm, tn), lambda i, j: (i, j)),
                  pl.BlockSpec((d, tn), lambda i, j: (0, j)),
                  pl.BlockSpec((pd, tn), lambda i, j: (0, j)),
                  pl.BlockSpec((1, d), lambda i, j: (0, 0))],
        out_specs=pl.BlockSpec((tm, d), lambda i, j: (i, 0)),
        compiler_params=_params(("parallel", "arbitrary")),
        name="ple_final",
    )(n3, p, h2, wg_bf, wp_bf, g_final.reshape(1, d))


def kernel(x_prompt, x_sample, state_hgrn, state_s5_re, state_s5_im, p_prompt, p_sample, g_mix, w_in, hgrn_gamma, g_onorm, s5_a_re, s5_a_im, s5_log_dt, s5_b_re, s5_b_im, s5_c_re, s5_c_im, s5_d, glu_w1, glu_b1, glu_w2, glu_b2, w_out, g_ffn, w_router, b_router, w_gate_up, b_gate_up, w_down, b_down, g_ple, w_ple_gate, w_ple_proj, g_final):
    assert w_in.shape[0] == 1, "single-layer trunk"
    bsz, T, d = x_prompt.shape
    nb = x_sample.shape[0]
    n_prompt = bsz * T
    heads = hgrn_gamma.shape[1] // HEAD_DIM
    G, P = s5_a_re.shape[1:]
    n_experts = w_router.shape[-1]
    u_col0 = 4 * heads

    lb = jnp.cumsum(jax.nn.softmax(hgrn_gamma.astype(F32), axis=0), axis=0)[0]
    h = jnp.concatenate([x_prompt.reshape(n_prompt, d), x_sample.reshape(nb, d)], axis=0)
    z = _in_proj(h, g_mix[0], w_in[0].astype(BF16))
    z_s = z[n_prompt:]
    oa_p, s_p = _hgrn_prompt(z, bsz, T, lb, g_onorm[0], heads)
    oa_s, s_s = _hgrn_sample(z_s, state_hgrn[0], lb, g_onorm[0], heads)
    tb = _s5_tables(s5_a_re[0], s5_a_im[0], s5_log_dt[0], s5_b_re[0], s5_b_im[0], s5_c_re[0], s5_c_im[0],
                    s5_d[0], glu_w1[0], glu_b1[0], glu_w2[0], glu_b2[0], S5_TT)
    ob_p, xr_p, xi_p = _s5_prompt(z, bsz, T, tb, u_col0)
    ob_s, xr_s, xi_s = _s5_sample(z_s, state_s5_re[0].reshape(nb, G * P), state_s5_im[0].reshape(nb, G * P),
                                  tb, u_col0)
    o_a = jnp.concatenate([oa_p.reshape(n_prompt, -1), oa_s], axis=0)
    o_b = jnp.concatenate([ob_p.reshape(n_prompt, -1), ob_s], axis=0)
    h1, tok, ti, tw = _out_proj(o_a, o_b, h, w_out[0].astype(BF16), g_ffn[0], w_router[0], b_router[0])
    meta, src_token, ipos, n_items = _moe_plan(ti[:, :TOP_K], n_experts)
    xs = tok.at[src_token].get(mode='promise_in_bounds')
    act = _moe_up(meta, xs, w_gate_up[0], b_gate_up[0], n_items)
    y_items = _moe_down(meta, act, w_down[0], b_down[0])
    y_sel = y_items.reshape(-1, d).at[ipos.reshape(-1)].get(mode='promise_in_bounds')
    h2, n3 = _combine_norm(h1, y_sel.reshape(TOP_K, -1, d), tw, g_ple[0])
    p_all = jnp.concatenate([p_prompt[0].reshape(n_prompt, -1), p_sample[0].reshape(nb, -1)], axis=0)
    y = _ple(n3, p_all, h2, w_ple_gate[0].astype(BF16), w_ple_proj[0].astype(BF16), g_final)
    return (y[:n_prompt].reshape(bsz, T, d), y[n_prompt:].reshape(nb, 1, d),
            s_p[None], xr_p.reshape(1, bsz, G, P), xi_p.reshape(1, bsz, G, P),
            s_s[None], xr_s.reshape(1, nb, G, P), xi_s.reshape(1, nb, G, P))
```
